```python
import jax, jax.numpy as jnp
from jax import lax
import numpy as np

D_MODEL = 1024
BATCH = 8
SEQ = 2048
DEPTH = 1
DEC_BATCH = 32
DEC_SEQ = 4
PAST_LEN = 8192
PAGE_SIZE = 128

D_PLE = 256
A_GROUPS = 4
A_WIDTH = 128
A_CHUNK = 128
A_DIM = A_GROUPS * A_WIDTH
B_HEADS = 8
B_HEAD_DIM = 64
B_DIM = B_HEADS * B_HEAD_DIM
MOBA_BLOCK = 256
MOBA_TOPK = 3
Q_BLOCK = 64
D_FF = 4 * D_MODEL
IN_DIM = 2 * A_DIM + 3 * B_DIM
EPS = 1e-6

kernel_name = "hymba_gmlp_moba_decode_step"


def rms_norm(x, g):
    xf = x.astype(jnp.float32)
    y = xf * lax.rsqrt(jnp.mean(xf * xf, axis=-1, keepdims=True) + EPS)
    return (y * g.astype(jnp.float32)).astype(x.dtype)


def in_projection(hn, w_in, a_v_norm, q_norm, k_norm):
    z = hn @ w_in
    u, va, q, k, vb = jnp.split(z, [A_DIM, 2 * A_DIM, 2 * A_DIM + B_DIM, 2 * A_DIM + 2 * B_DIM], axis=-1)
    lead = hn.shape[:-1]
    u = jax.nn.gelu(u).reshape(*lead, A_GROUPS, A_WIDTH)
    va = rms_norm(jax.nn.gelu(va).reshape(*lead, A_GROUPS, A_WIDTH), a_v_norm)
    q = rms_norm(q.reshape(*lead, B_HEADS, B_HEAD_DIM), q_norm)
    k = rms_norm(k.reshape(*lead, B_HEADS, B_HEAD_DIM), k_norm)
    vb = vb.reshape(*lead, B_HEADS, B_HEAD_DIM)
    return u, va, q, k, vb


def spatial_gate_prompt(u, va, ws, bs):
    b, s = u.shape[:2]
    nc = s // A_CHUNK
    w = jnp.tril(ws)
    vc = va.reshape(b, nc, A_CHUNK, A_GROUPS, A_WIDTH)
    sg = jnp.einsum('gts,bcsgw->bctgw', w, vc) + bs.T[:, :, None]
    return (u * sg.reshape(b, s, A_GROUPS, A_WIDTH)).reshape(b, s, A_DIM)


def spatial_gate_sample(u, va, ws, bs):
    b, n = u.shape[:2]
    w = jnp.tril(ws)[:, :n, :n]
    sg = jnp.einsum('gts,bsgw->btgw', w, va) + bs[:, :n].T[:, :, None]
    return (u * sg).reshape(b, n, A_DIM)


def to_blocks(x):
    b, l = x.shape[:2]
    nb = -(-l // MOBA_BLOCK)
    x = jnp.pad(x, ((0, 0), (0, nb * MOBA_BLOCK - l), (0, 0), (0, 0)))
    return x.reshape(b, nb, MOBA_BLOCK, B_HEADS, B_HEAD_DIM).transpose(0, 3, 1, 2, 4)


def moba_queries(q, t, kb, vb, kmean):
    b, h, nq, d = q.shape
    nb = kb.shape[2]
    k_sel = min(MOBA_TOPK, nb)
    n_past = t // MOBA_BLOCK
    gate = jnp.einsum('bhqd,bhnd->bhqn', q.astype(jnp.float32), kmean)
    past = jnp.arange(nb)[None, :] < n_past[:, None]
    gate = jnp.where(past[None, None], gate, -jnp.inf)
    _, top = lax.top_k(gate, k_sel)
    own = jnp.broadcast_to(n_past[None, None, :, None], (b, h, nq, 1)).astype(top.dtype)
    sel = jnp.concatenate([top, own], axis=-1)
    slot = jnp.arange(k_sel + 1)
    slot_ok = (slot[None, :] < n_past[:, None]) | (slot[None, :] == k_sel)
    bi = jnp.arange(b)[:, None, None, None]
    hi = jnp.arange(h)[None, :, None, None]
    kg = kb[bi, hi, sel]
    vg = vb[bi, hi, sel]
    pos = sel[..., None] * MOBA_BLOCK + jnp.arange(MOBA_BLOCK)
    mask = slot_ok[None, None, :, :, None] & (pos <= t[None, None, :, None, None])
    s = jnp.einsum('bhqd,bhqnkd->bhqnk', q, kg).astype(jnp.float32) * (d ** -0.5)
    s = jnp.where(mask, s, -jnp.inf)
    p = jax.nn.softmax(s.reshape(b, h, nq, -1), axis=-1).reshape(s.shape).astype(vg.dtype)
    return jnp.einsum('bhqnk,bhqnkd->bhqd', p, vg)


def moba_prompt(q, k, v):
    b, s = q.shape[:2]
    kb, vb = to_blocks(k), to_blocks(v)
    kmean = jnp.mean(kb.astype(jnp.float32), axis=3)
    qt = q.transpose(0, 2, 1, 3)
    nc = s // Q_BLOCK

    def chunk(c):
        start = c * Q_BLOCK
        qc = lax.dynamic_slice_in_dim(qt, start, Q_BLOCK, axis=2)
        t = start + jnp.arange(Q_BLOCK, dtype=jnp.int32)
        return moba_queries(qc, t, kb, vb, kmean)

    o = lax.map(chunk, jnp.arange(nc, dtype=jnp.int32))
    return o.transpose(1, 0, 3, 2, 4).reshape(b, s, B_DIM)


def moba_sample(q, k, v, cache_k, cache_v, page_table):
    db, n = q.shape[:2]
    past_k = cache_k[page_table].reshape(db, -1, B_HEADS, B_HEAD_DIM).astype(k.dtype)
    past_v = cache_v[page_table].reshape(db, -1, B_HEADS, B_HEAD_DIM).astype(v.dtype)
    past_len = past_k.shape[1]
    kb = to_blocks(jnp.concatenate([past_k, k], axis=1))
    vb = to_blocks(jnp.concatenate([past_v, v], axis=1))
    kmean = jnp.mean(kb.astype(jnp.float32), axis=3)
    t = past_len + jnp.arange(n, dtype=jnp.int32)
    o = moba_queries(q.transpose(0, 2, 1, 3), t, kb, vb, kmean)
    return o.transpose(0, 2, 1, 3).reshape(db, n, B_DIM)


def channel_and_ple(h, p, ln2, w_up, w_down, ln3, w_ple_gate, w_ple_proj, ple_norm):
    h = h + jnp.square(jax.nn.relu(rms_norm(h, ln2) @ w_up)) @ w_down
    gate = jax.nn.sigmoid(rms_norm(h, ln3) @ w_ple_gate)
    e = rms_norm(p @ w_ple_proj, ple_norm)
    return h + gate * e


def setup_inputs(seed: int = 0) -> dict:
    key = jax.random.key(seed)
    ks = jax.random.split(key, 24)
    n_pages = PAST_LEN // PAGE_SIZE
    n_used = DEC_BATCH * n_pages
    n_pool = n_used + (n_used + 3) // 4
    f32 = jnp.float32
    nrm = lambda k, shp, sc: jax.random.normal(k, shp, f32) * sc
    gain = lambda k, shp: 1.0 + 0.05 * jax.random.normal(k, shp, f32)
    page_table = jax.random.permutation(ks[6], n_pool)[:n_used].reshape(DEC_BATCH, n_pages).astype(jnp.int32)
    return {
        "x_prompt": nrm(ks[0], (BATCH, SEQ, D_MODEL), 1.0),
        "x_sample": nrm(ks[1], (DEC_BATCH, DEC_SEQ, D_MODEL), 1.0),
        "p_prompt": nrm(ks[2], (DEPTH, BATCH, SEQ, D_PLE), 1.0),
        "p_sample": nrm(ks[3], (DEPTH, DEC_BATCH, DEC_SEQ, D_PLE), 1.0),
        "cache_k": nrm(ks[4], (DEPTH, n_pool, PAGE_SIZE, B_HEADS, B_HEAD_DIM), 1.0),
        "cache_v": nrm(ks[5], (DEPTH, n_pool, PAGE_SIZE, B_HEADS, B_HEAD_DIM), 1.0),
        "page_table": page_table,
        "ln1": gain(ks[7], (DEPTH, D_MODEL)),
        "w_in": nrm(ks[8], (DEPTH, D_MODEL, IN_DIM), D_MODEL ** -0.5),
        "a_v_norm": gain(ks[9], (DEPTH, A_GROUPS, A_WIDTH)),
        "a_ws": nrm(ks[10], (DEPTH, A_GROUPS, A_CHUNK, A_CHUNK), A_CHUNK ** -0.5),
        "a_bs": gain(ks[11], (DEPTH, A_GROUPS, A_CHUNK)),
        "q_norm": gain(ks[12], (DEPTH, B_HEAD_DIM)),
        "k_norm": gain(ks[13], (DEPTH, B_HEAD_DIM)),
        "w_out": nrm(ks[14], (DEPTH, A_DIM + B_DIM, D_MODEL), (A_DIM + B_DIM) ** -0.5),
        "ln2": gain(ks[15], (DEPTH, D_MODEL)),
        "w_up": nrm(ks[16], (DEPTH, D_MODEL, D_FF), D_MODEL ** -0.5),
        "w_down": nrm(ks[17], (DEPTH, D_FF, D_MODEL), D_FF ** -0.5),
        "ln3": gain(ks[18], (DEPTH, D_MODEL)),
        "w_ple_gate": nrm(ks[19], (DEPTH, D_MODEL, D_MODEL), D_MODEL ** -0.5),
        "w_ple_proj": nrm(ks[20], (DEPTH, D_PLE, D_MODEL), D_PLE ** -0.5),
        "ple_norm": gain(ks[21], (DEPTH, D_MODEL)),
    }


def reference(x_prompt, x_sample, p_prompt, p_sample, cache_k, cache_v, page_table,
              ln1, w_in, a_v_norm, a_ws, a_bs, q_norm, k_norm, w_out,
              ln2, w_up, w_down, ln3, w_ple_gate, w_ple_proj, ple_norm):
    hp, hs = x_prompt, x_sample
    kp_l, vp_l, ks_l, vs_l, as_l = [], [], [], [], []
    for i in range(DEPTH):
        u, va, q, k, v = in_projection(rms_norm(hp, ln1[i]), w_in[i], a_v_norm[i], q_norm[i], k_norm[i])
        mix = jnp.concatenate([spatial_gate_prompt(u, va, a_ws[i], a_bs[i]), moba_prompt(q, k, v)], axis=-1)
        hp = channel_and_ple(hp + mix @ w_out[i], p_prompt[i], ln2[i], w_up[i], w_down[i],
                             ln3[i], w_ple_gate[i], w_ple_proj[i], ple_norm[i])
        kp_l.append(k)
        vp_l.append(v)
        u, va, q, k, v = in_projection(rms_norm(hs, ln1[i]), w_in[i], a_v_norm[i], q_norm[i], k_norm[i])
        mix = jnp.concatenate([spatial_gate_sample(u, va, a_ws[i], a_bs[i]),
                               moba_sample(q, k, v, cache_k[i], cache_v[i], page_table)], axis=-1)
        hs = channel_and_ple(hs + mix @ w_out[i], p_sample[i], ln2[i], w_up[i], w_down[i],
                             ln3[i], w_ple_gate[i], w_ple_proj[i], ple_norm[i])
        ks_l.append(k)
        vs_l.append(v)
        as_l.append(va)
    k_prompt_new = jnp.stack(kp_l)
    v_prompt_new = jnp.stack(vp_l)
    k_sample_new = jnp.stack(ks_l)
    v_sample_new = jnp.stack(vs_l)
    state_a_v_sample = jnp.stack(as_l)
    return (hp, hs, k_prompt_new, v_prompt_new, k_sample_new, v_sample_new, state_a_v_sample)
```

```python
import functools

import jax
import jax.numpy as jnp
from jax import lax
from jax.experimental import pallas as pl
from jax.experimental.pallas import tpu as pltpu

D_MODEL = 1024
D_PLE = 256
A_GROUPS = 4
A_WIDTH = 128
A_CHUNK = 128
A_DIM = A_GROUPS * A_WIDTH
B_HEADS = 8
B_HEAD_DIM = 64
B_DIM = B_HEADS * B_HEAD_DIM
MOBA_BLOCK = 256
MOBA_TOPK = 3
D_FF = 4 * D_MODEL
IN_DIM = 2 * A_DIM + 3 * B_DIM
EPS = 1e-6

LANES = 128
ROW_TILE = 512
FF_CHUNK = 1024
PAGES_PER_STEP = 8
VMEM_LIMIT = 56 * 1024 * 1024

BF16 = jnp.bfloat16
F32 = jnp.float32
NEG_INF = float("-inf")


def _dot(a, b):
    return jnp.dot(a, b, preferred_element_type=F32)


def _dot_nt(a, b):
    return lax.dot_general(a, b, (((1,), (1,)), ((), ())), preferred_element_type=F32)


def _rms(x, gain):
    return x * lax.rsqrt(jnp.mean(x * x, axis=-1, keepdims=True) + EPS) * gain


def _const_spec(shape):
    nd = len(shape)
    return pl.BlockSpec(shape, lambda *_: (0,) * nd, pipeline_mode=pl.Buffered(1))


def _head_rms(z, gain_row):
    lane = lax.broadcasted_iota(jnp.int32, (1, LANES), 1)
    lo = lane < B_HEAD_DIM
    outs = []
    for t in range(B_DIM // LANES):
        zt = z[:, t * LANES:(t + 1) * LANES]
        z2 = zt * zt
        s_lo = jnp.sum(jnp.where(lo, z2, 0.0), axis=-1, keepdims=True)
        s_hi = jnp.sum(jnp.where(lo, 0.0, z2), axis=-1, keepdims=True)
        ms = jnp.where(lo, s_lo, s_hi) * (1.0 / B_HEAD_DIM)
        outs.append(zt * lax.rsqrt(ms + EPS))
    return jnp.concatenate(outs, axis=-1) * gain_row


def _in_proj_kernel(x_ref, ln1_ref, w_ref, avn_ref, qn_ref, kn_ref, mixw_ref, mixb_ref,
                    *out_refs, mix_block, emit_va):
    if emit_va:
        mixa_ref, q_ref, k_ref, v_ref, va_ref = out_refs
    else:
        mixa_ref, q_ref, k_ref, v_ref = out_refs
    rows = x_ref.shape[0]
    hn = _rms(x_ref[...], ln1_ref[...]).astype(BF16)

    def proj(c0, n):
        return _dot(hn, w_ref[:, c0:c0 + n])

    u = jax.nn.gelu(proj(0, A_DIM))
    gv = jax.nn.gelu(proj(A_DIM, A_DIM))
    va = jnp.concatenate(
        [_rms(gv[:, g * A_WIDTH:(g + 1) * A_WIDTH], 1.0) for g in range(A_GROUPS)], axis=-1) * avn_ref[...]
    if emit_va:
        va_ref[...] = va

    r = lax.broadcasted_iota(jnp.int32, (A_CHUNK, A_CHUNK), 0)
    c = lax.broadcasted_iota(jnp.int32, (A_CHUNK, A_CHUNK), 1)
    keep = (r // mix_block == c // mix_block) & (c % mix_block <= r % mix_block)
    va16 = va.astype(BF16)
    for g in range(A_GROUPS):
        wg = jnp.where(keep, mixw_ref[g], 0.0).astype(BF16)
        bg = mixb_ref[g]
        for ch in range(rows // A_CHUNK):
            rs = slice(ch * A_CHUNK, (ch + 1) * A_CHUNK)
            cs = slice(g * A_WIDTH, (g + 1) * A_WIDTH)
            sg = _dot(wg, va16[rs, cs]) + bg
            mixa_ref[rs, cs] = (u[rs, cs] * sg).astype(BF16)

    q0 = 2 * A_DIM
    q = _head_rms(proj(q0, B_DIM), qn_ref[...])
    q_ref[...] = (q * (B_HEAD_DIM ** -0.5)).astype(BF16)
    k_ref[...] = _head_rms(proj(q0 + B_DIM, B_DIM), kn_ref[...])
    v_ref[...] = proj(q0 + 2 * B_DIM, B_DIM)


def _in_proj(x, ln1, w_in16, avn, qn, kn, mixw, mixb, *, rows_per_step, mix_block, emit_va):
    n = x.shape[0]
    row = lambda width: pl.BlockSpec((rows_per_step, width), lambda i: (i, 0))
    out_shape = [jax.ShapeDtypeStruct((n, A_DIM), BF16), jax.ShapeDtypeStruct((n, B_DIM), BF16),
                 jax.ShapeDtypeStruct((n, B_DIM), F32), jax.ShapeDtypeStruct((n, B_DIM), F32)]
    out_specs = [row(A_DIM), row(B_DIM), row(B_DIM), row(B_DIM)]
    if emit_va:
        out_shape.append(jax.ShapeDtypeStruct((n, A_DIM), F32))
        out_specs.append(row(A_DIM))
    return pl.pallas_call(
        functools.partial(_in_proj_kernel, mix_block=mix_block, emit_va=emit_va),
        grid=(n // rows_per_step,),
        in_specs=[row(D_MODEL), _const_spec((1, D_MODEL)), _const_spec((D_MODEL, IN_DIM)),
                  _const_spec((1, A_DIM)), _const_spec((1, B_DIM)), _const_spec((1, B_DIM)),
                  _const_spec((A_GROUPS, A_CHUNK, A_CHUNK)), _const_spec((A_GROUPS, A_CHUNK, A_WIDTH))],
        out_specs=out_specs,
        out_shape=out_shape,
        compiler_params=pltpu.CompilerParams(dimension_semantics=("arbitrary",), vmem_limit_bytes=VMEM_LIMIT),
        name="in_proj_sample" if emit_va else "in_proj_prompt",
    )(x, ln1, w_in16, avn, qn, kn, mixw, mixb)


def _top_k_mask(g, valid, n_cand):
    lane = lax.broadcasted_iota(jnp.int32, g.shape, 1)
    gm = jnp.where(valid, g, NEG_INF)
    rank = jnp.zeros(g.shape, jnp.int32)
    for m in range(n_cand):
        col = gm[:, m:m + 1]
        beats = (col > gm) | ((col == gm) & (lane > m))
        rank = rank + beats.astype(jnp.int32)
    return valid & (rank < MOBA_TOPK)


def _moba_prompt_kernel(q_ref, k_ref, v_ref, o_ref, k16_ref, v16_ref):
    seq = q_ref.shape[0]
    n_blocks = seq // MOBA_BLOCK
    k16_ref[...] = k_ref[...].astype(BF16)
    v16_ref[...] = v_ref[...].astype(BF16)
    kmean = jnp.concatenate(
        [jnp.mean(k_ref[j * MOBA_BLOCK:(j + 1) * MOBA_BLOCK, :], axis=0, keepdims=True) for j in range(n_blocks)]
        + [jnp.zeros((LANES - n_blocks, LANES), F32)], axis=0).astype(BF16)

    lane_q = lax.broadcasted_iota(jnp.int32, (MOBA_BLOCK, LANES), 1)
    first_head = lane_q < B_HEAD_DIM
    kpos = lax.broadcasted_iota(jnp.int32, (MOBA_BLOCK, MOBA_BLOCK), 1)
    qpos = lax.broadcasted_iota(jnp.int32, (MOBA_BLOCK, MOBA_BLOCK), 0)
    causal = kpos <= qpos

    def q_tile(i, _):
        rows = pl.ds(pl.multiple_of(i * MOBA_BLOCK, MOBA_BLOCK), MOBA_BLOCK)
        q_pair = q_ref[rows, :]
        k_own = k16_ref[rows, :]
        v_own = v16_ref[rows, :]
        outs = []
        for head_mask in (first_head, ~first_head):
            qh = jnp.where(head_mask, q_pair, jnp.zeros_like(q_pair))
            sel = _top_k_mask(_dot_nt(qh, kmean), lane_q < i, n_blocks - 1).astype(F32)

            s = jnp.where(causal, _dot_nt(qh, k_own), NEG_INF)
            m0 = jnp.max(s, axis=-1, keepdims=True)
            p = jnp.exp(s - m0)
            l0 = jnp.sum(p, axis=-1, keepdims=True)
            acc0 = _dot(p.astype(BF16), v_own)

            def past_block(j, carry):
                m_prev, l_prev, acc = carry
                rows_j = pl.ds(pl.multiple_of(j * MOBA_BLOCK, MOBA_BLOCK), MOBA_BLOCK)
                sel_j = jnp.sum(jnp.where(lane_q == j, sel, 0.0), axis=-1, keepdims=True) > 0.0
                s = jnp.where(sel_j, _dot_nt(qh, k16_ref[rows_j, :]), NEG_INF)
                m_new = jnp.maximum(m_prev, jnp.max(s, axis=-1, keepdims=True))
                alpha = jnp.exp(m_prev - m_new)
                p = jnp.exp(s - m_new)
                l_new = alpha * l_prev + jnp.sum(p, axis=-1, keepdims=True)
                acc = alpha * acc + _dot(p.astype(BF16), v16_ref[rows_j, :])
                return m_new, l_new, acc

            _, l_fin, acc = lax.fori_loop(0, i, past_block, (m0, l0, acc0))
            outs.append(acc / l_fin)
        o_ref[rows, :] = jnp.where(first_head, outs[0], outs[1]).astype(o_ref.dtype)
        return 0

    lax.fori_loop(0, n_blocks, q_tile, 0)


def _moba_prompt(q, k, v, *, batch, seq):
    spec = pl.BlockSpec((seq, LANES), lambda b, hp: (b, hp))
    return pl.pallas_call(
        _moba_prompt_kernel,
        grid=(batch, B_DIM // LANES),
        in_specs=[spec, spec, spec],
        out_specs=spec,
        out_shape=jax.ShapeDtypeStruct((batch * seq, B_DIM), BF16),
        scratch_shapes=[pltpu.VMEM((seq, LANES), BF16), pltpu.VMEM((seq, LANES), BF16)],
        compiler_params=pltpu.CompilerParams(dimension_semantics=("arbitrary", "arbitrary"),
                                             vmem_limit_bytes=VMEM_LIMIT),
        name="moba_prompt",
    )(q, k, v)


def _moba_sample_kernel(pt_ref, qrep_ref, knew_ref, vnew_ref, *refs, n_past_blocks, page):
    del pt_ref
    k_pages = refs[:PAGES_PER_STEP]
    v_pages = refs[PAGES_PER_STEP:2 * PAGES_PER_STEP]
    o_ref, s_ref, kv16_ref, acc_ref, l_ref = refs[2 * PAGES_PER_STEP:]
    j = pl.program_id(1)
    k_steps = n_past_blocks * MOBA_BLOCK // (page * PAGES_PER_STEP)
    n_rows = qrep_ref.shape[0]
    blocks_per_step = PAGES_PER_STEP * page // MOBA_BLOCK

    row = lax.broadcasted_iota(jnp.int32, (n_rows, B_DIM), 0)
    col = lax.broadcasted_iota(jnp.int32, (n_rows, B_DIM), 1)
    own_head = col // B_HEAD_DIM == row % B_HEADS
    q_heads = jnp.where(own_head, qrep_ref[...], jnp.zeros((n_rows, B_DIM), BF16))

    def stage_pages(pages):
        for r in range(PAGES_PER_STEP):
            kv16_ref[:, r * page:(r + 1) * page] = pages[r][...].astype(BF16)

    def block_cols(n):
        c0 = (n % blocks_per_step) * MOBA_BLOCK
        return n // blocks_per_step, slice(c0, c0 + MOBA_BLOCK)

    @pl.when(j < k_steps)
    def _scores():
        stage_pages(k_pages)
        s_ref[j] = _dot(q_heads, kv16_ref[...])

    @pl.when(j == k_steps)
    def _select():
        lane = lax.broadcasted_iota(jnp.int32, (n_rows, LANES), 1)
        qrow = lax.broadcasted_iota(jnp.int32, (n_rows, LANES), 0) // B_HEADS
        gate = jnp.zeros((n_rows, LANES), F32)
        for n in range(n_past_blocks):
            st, cols = block_cols(n)
            gate = jnp.where(lane == n, jnp.sum(s_ref[st, :, cols], axis=-1, keepdims=True), gate)
        sel = _top_k_mask(gate, lane < n_past_blocks, n_past_blocks).astype(F32)

        k_own = jnp.concatenate([knew_ref[...], jnp.zeros((LANES - knew_ref.shape[0], B_DIM), F32)], axis=0)
        own_ok = lane <= qrow
        s_own = jnp.where(own_ok, _dot_nt(q_heads, k_own.astype(BF16)), NEG_INF)

        sel_cols = [sel[:, n:n + 1] > 0.0 for n in range(n_past_blocks)]
        m_run = jnp.full((n_rows, MOBA_BLOCK), NEG_INF, F32)
        for n in range(n_past_blocks):
            st, cols = block_cols(n)
            m_run = jnp.maximum(m_run, jnp.where(sel_cols[n], s_ref[st, :, cols], NEG_INF))
        m = jnp.maximum(jnp.max(m_run, axis=-1, keepdims=True), jnp.max(s_own, axis=-1, keepdims=True))

        p_own = jnp.exp(s_own - m)
        l_run = jnp.zeros((n_rows, MOBA_BLOCK), F32)
        for n in range(n_past_blocks):
            st, cols = block_cols(n)
            pn = jnp.where(sel_cols[n], jnp.exp(s_ref[st, :, cols] - m), 0.0)
            s_ref[st, :, cols] = pn
            l_run = l_run + pn
        l_ref[...] = jnp.broadcast_to(
            jnp.sum(l_run, axis=-1, keepdims=True) + jnp.sum(p_own, axis=-1, keepdims=True), l_ref.shape)
        acc = jnp.zeros((n_rows, B_DIM), F32)
        for t in range(vnew_ref.shape[0]):
            acc = acc + p_own[:, t:t + 1] * vnew_ref[t:t + 1, :]
        acc_ref[...] = acc

    @pl.when(j >= k_steps)
    def _values():
        stage_pages(v_pages)
        acc_ref[...] += _dot_nt(s_ref[j - k_steps].astype(BF16), kv16_ref[...])

    @pl.when(j == 2 * k_steps - 1)
    def _finish():
        out = jnp.where(own_head, acc_ref[...] / l_ref[:, 0:1], 0.0)
        n_q = n_rows // B_HEADS
        rows = [jnp.sum(out[t * B_HEADS:(t + 1) * B_HEADS, :], axis=0, keepdims=True) for t in range(n_q)]
        o_ref[...] = jnp.concatenate(rows + [jnp.zeros((o_ref.shape[0] - n_q, B_DIM), F32)], axis=0)


def _moba_sample(page_table, qrep, knew, vnew, cache_k, cache_v, *, n_new):
    dec_batch, n_pages = page_table.shape
    page = cache_k.shape[2]
    past_len = n_pages * page
    n_past_blocks = past_len // MOBA_BLOCK
    k_steps = n_pages // PAGES_PER_STEP
    n_rows = n_new * B_HEADS
    new_pad = knew.shape[1]

    def k_page(r):
        return pl.BlockSpec((None, B_DIM, page),
                            lambda b, j, pt: (pt[b, jnp.minimum(j, k_steps - 1) * PAGES_PER_STEP + r], 0, 0))

    def v_page(r):
        return pl.BlockSpec((None, B_DIM, page),
                            lambda b, j, pt: (pt[b, jnp.maximum(j - k_steps, 0) * PAGES_PER_STEP + r], 0, 0))

    per_batch = lambda rows: pl.BlockSpec((None, rows, B_DIM), lambda b, j, pt: (b, 0, 0))
    grid_spec = pltpu.PrefetchScalarGridSpec(
        num_scalar_prefetch=1,
        grid=(dec_batch, 2 * k_steps),
        in_specs=[per_batch(n_rows), per_batch(new_pad), per_batch(new_pad)]
        + [k_page(r) for r in range(PAGES_PER_STEP)] + [v_page(r) for r in range(PAGES_PER_STEP)],
        out_specs=per_batch(new_pad),
        scratch_shapes=[pltpu.VMEM((k_steps, n_rows, PAGES_PER_STEP * page), F32),
                        pltpu.VMEM((B_DIM, PAGES_PER_STEP * page), BF16),
                        pltpu.VMEM((n_rows, B_DIM), F32), pltpu.VMEM((n_rows, LANES), F32)],
    )
    return pl.pallas_call(
        functools.partial(_moba_sample_kernel, n_past_blocks=n_past_blocks, page=page),
        grid_spec=grid_spec,
        out_shape=jax.ShapeDtypeStruct((dec_batch, new_pad, B_DIM), F32),
        compiler_params=pltpu.CompilerParams(dimension_semantics=("arbitrary", "arbitrary"),
                                             vmem_limit_bytes=VMEM_LIMIT),
        name="moba_sample",
    )(page_table, qrep, knew, vnew, *([cache_k] * PAGES_PER_STEP), *([cache_v] * PAGES_PER_STEP))


def _mlp_ple_kernel(x_ref, ma_ref, mb_ref, p_ref, wout_ref, ln2_ref, wup_ref, wdn_ref, ln3_ref, wg_ref,
                    wpp_ref, pn_ref, y_ref, mix_ref, act_ref):
    mix_ref[:, 0:A_DIM] = ma_ref[...]
    mix_ref[:, A_DIM:A_DIM + B_DIM] = mb_ref[...]
    h = x_ref[...] + _dot(mix_ref[...], wout_ref[...])
    hn = _rms(h, ln2_ref[...]).astype(BF16)
    for c0 in range(0, D_FF, FF_CHUNK):
        act_ref[:, c0:c0 + FF_CHUNK] = jnp.square(jnp.maximum(_dot(hn, wup_ref[:, c0:c0 + FF_CHUNK]), 0.0)).astype(BF16)
    h = h + _dot(act_ref[...], wdn_ref[...])
    gate = 1.0 / (1.0 + jnp.exp(-_dot(_rms(h, ln3_ref[...]).astype(BF16), wg_ref[...])))
    e = _rms(_dot(p_ref[...].astype(BF16), wpp_ref[...]), pn_ref[...])
    y_ref[...] = h + gate * e


def _mlp_ple(x, mix_a, mix_b, p, w_out16, ln2, w_up16, w_down16, ln3, w_gate16, w_proj16, ple_norm, *,
             rows_per_step, name):
    n = x.shape[0]
    row = lambda width: pl.BlockSpec((rows_per_step, width), lambda i: (i, 0))
    return pl.pallas_call(
        _mlp_ple_kernel,
        grid=(n // rows_per_step,),
        in_specs=[row(D_MODEL), row(A_DIM), row(B_DIM), row(D_PLE),
                  _const_spec((A_DIM + B_DIM, D_MODEL)), _const_spec((1, D_MODEL)),
                  _const_spec((D_MODEL, D_FF)), _const_spec((D_FF, D_MODEL)), _const_spec((1, D_MODEL)),
                  _const_spec((D_MODEL, D_MODEL)), _const_spec((D_PLE, D_MODEL)), _const_spec((1, D_MODEL))],
        out_specs=row(D_MODEL),
        out_shape=jax.ShapeDtypeStruct((n, D_MODEL), F32),
        scratch_shapes=[pltpu.VMEM((rows_per_step, A_DIM + B_DIM), BF16), pltpu.VMEM((rows_per_step, D_FF), BF16)],
        compiler_params=pltpu.CompilerParams(dimension_semantics=("arbitrary",), vmem_limit_bytes=VMEM_LIMIT),
        name=name,
    )(x, mix_a, mix_b, p, w_out16, ln2, w_up16, w_down16, ln3, w_gate16, w_proj16, ple_norm)


def kernel(x_prompt, x_sample, p_prompt, p_sample, cache_k, cache_v, page_table, ln1, w_in, a_v_norm, a_ws, a_bs,
           q_norm, k_norm, w_out, ln2, w_up, w_down, ln3, w_ple_gate, w_ple_proj, ple_norm):
    depth = ln1.shape[0]
    batch, seq, _ = x_prompt.shape
    dec_batch, n_new, _ = x_sample.shape
    n_pool, page = cache_k.shape[1], cache_k.shape[2]
    n_p, n_s = batch * seq, dec_batch * n_new
    assert n_s == A_CHUNK and n_new <= A_CHUNK and A_CHUNK % n_new == 0 and seq % ROW_TILE == 0
    assert MOBA_BLOCK % page == 0 and (page_table.shape[1] * page) % MOBA_BLOCK == 0

    hp = x_prompt.reshape(n_p, D_MODEL)
    hs = x_sample.reshape(n_s, D_MODEL)
    kp_l, vp_l, ks_l, vs_l, as_l = [], [], [], [], []
    for i in range(depth):
        row = lambda a: a.reshape(1, -1)
        w_in16, w_out16, w_up16, w_down16 = (w[i].astype(BF16) for w in (w_in, w_out, w_up, w_down))
        w_gate16, w_proj16 = w_ple_gate[i].astype(BF16), w_ple_proj[i].astype(BF16)
        avn = row(a_v_norm[i])
        qn, kn = row(jnp.tile(q_norm[i], B_HEADS)), row(jnp.tile(k_norm[i], B_HEADS))
        reps = A_CHUNK // n_new
        bias_p = jnp.broadcast_to(a_bs[i][:, :, None], (A_GROUPS, A_CHUNK, A_WIDTH))
        mixw_s = jnp.tile(a_ws[i][:, :n_new, :n_new], (1, reps, reps))
        bias_s = jnp.broadcast_to(jnp.tile(a_bs[i][:, :n_new], (1, reps))[:, :, None], (A_GROUPS, A_CHUNK, A_WIDTH))

        mix_a, q, k, v = _in_proj(hp, row(ln1[i]), w_in16, avn, qn, kn, a_ws[i], bias_p,
                                  rows_per_step=ROW_TILE, mix_block=A_CHUNK, emit_va=False)
        mix_b = _moba_prompt(q, k, v, batch=batch, seq=seq)
        hp = _mlp_ple(hp, mix_a, mix_b, p_prompt[i].reshape(n_p, D_PLE), w_out16, row(ln2[i]), w_up16, w_down16,
                      row(ln3[i]), w_gate16, w_proj16, row(ple_norm[i]), rows_per_step=ROW_TILE, name="mlp_ple_prompt")
        kp_l.append(k.reshape(batch, seq, B_HEADS, B_HEAD_DIM))
        vp_l.append(v.reshape(batch, seq, B_HEADS, B_HEAD_DIM))

        mix_a, q, k, v, va = _in_proj(hs, row(ln1[i]), w_in16, avn, qn, kn, mixw_s, bias_s,
                                      rows_per_step=n_s, mix_block=n_new, emit_va=True)
        new_pad = 8
        pad_new = lambda a: jnp.pad(a.reshape(dec_batch, n_new, B_DIM), ((0, 0), (0, new_pad - n_new), (0, 0)))
        qrep = jnp.repeat(q.reshape(dec_batch, n_new, B_DIM), B_HEADS, axis=1)
        pages_t = lambda c: c.transpose(0, 2, 3, 1).reshape(n_pool, B_DIM, page)
        o = _moba_sample(page_table, qrep, pad_new(k), pad_new(v), pages_t(cache_k[i]), pages_t(cache_v[i]),
                         n_new=n_new)
        mix_b = o[:, :n_new, :].reshape(n_s, B_DIM).astype(BF16)
        hs = _mlp_ple(hs, mix_a, mix_b, p_sample[i].reshape(n_s, D_PLE), w_out16, row(ln2[i]), w_up16, w_down16,
                      row(ln3[i]), w_gate16, w_proj16, row(ple_norm[i]), rows_per_step=n_s, name="mlp_ple_sample")
        ks_l.append(k.reshape(dec_batch, n_new, B_HEADS, B_HEAD_DIM))
        vs_l.append(v.reshape(dec_batch, n_new, B_HEADS, B_HEAD_DIM))
        as_l.append(va.reshape(dec_batch, n_new, A_GROUPS, A_WIDTH))

    return (hp.reshape(batch, seq, D_MODEL), hs.reshape(dec_batch, n_new, D_MODEL),
            jnp.stack(kp_l), jnp.stack(vp_l), jnp.stack(ks_l), jnp.stack(vs_l), jnp.stack(as_l))
```

```python
import functools

import jax
import jax.numpy as jnp
from jax import lax
from jax.experimental import pallas as pl
from jax.experimental.pallas import tpu as pltpu

D_MODEL = 1024
D_PLE = 256
A_GROUPS = 4
A_WIDTH = 128
A_CHUNK = 128
A_DIM = A_GROUPS * A_WIDTH
B_HEADS = 8
B_HEAD_DIM = 64
B_DIM = B_HEADS * B_HEAD_DIM
MOBA_BLOCK = 256
MOBA_TOPK = 3
D_FF = 4 * D_MODEL
IN_DIM = 2 * A_DIM + 3 * B_DIM
EPS = 1e-6

LANES = 128
ROW_TILE = 512
FF_CHUNK = 1024
PAGES_PER_STEP = 8
VMEM_LIMIT = 56 * 1024 * 1024

BF16 = jnp.bfloat16
F32 = jnp.float32
NEG_INF = float("-inf")
MASK_BIAS = -1e30


def _dot(a, b):
    return jnp.dot(a, b, preferred_element_type=F32)


def _dot_nt(a, b):
    return lax.dot_general(a, b, (((1,), (1,)), ((), ())), preferred_element_type=F32)


def _rms(x, gain):
    return x * lax.rsqrt(jnp.mean(x * x, axis=-1, keepdims=True) + EPS) * gain


def _const_spec(shape):
    nd = len(shape)
    return pl.BlockSpec(shape, lambda *_: (0,) * nd, pipeline_mode=pl.Buffered(1))


def _head_rms(z, gain_row):
    lane = lax.broadcasted_iota(jnp.int32, (1, LANES), 1)
    lo = lane < B_HEAD_DIM
    outs = []
    for t in range(B_DIM // LANES):
        zt = z[:, t * LANES:(t + 1) * LANES]
        z2 = zt * zt
        s_lo = jnp.sum(jnp.where(lo, z2, 0.0), axis=-1, keepdims=True)
        s_hi = jnp.sum(jnp.where(lo, 0.0, z2), axis=-1, keepdims=True)
        ms = jnp.where(lo, s_lo, s_hi) * (1.0 / B_HEAD_DIM)
        outs.append(zt * lax.rsqrt(ms + EPS))
    return jnp.concatenate(outs, axis=-1) * gain_row


def _in_proj_kernel(x_ref, ln1_ref, w_ref, avn_ref, qn_ref, kn_ref, mixw_ref, mixb_ref,
                    *out_refs, mix_block, emit_va):
    if emit_va:
        mixa_ref, q_ref, k_ref, v_ref, va_ref = out_refs
    else:
        mixa_ref, q_ref, k_ref, v_ref = out_refs
    rows = x_ref.shape[0]
    hn = _rms(x_ref[...], ln1_ref[...]).astype(BF16)

    def proj(c0, n):
        return _dot(hn, w_ref[:, c0:c0 + n])

    u = jax.nn.gelu(proj(0, A_DIM))
    gv = jax.nn.gelu(proj(A_DIM, A_DIM))
    va = jnp.concatenate(
        [_rms(gv[:, g * A_WIDTH:(g + 1) * A_WIDTH], 1.0) for g in range(A_GROUPS)], axis=-1) * avn_ref[...]
    if emit_va:
        va_ref[...] = va

    r = lax.broadcasted_iota(jnp.int32, (A_CHUNK, A_CHUNK), 0)
    c = lax.broadcasted_iota(jnp.int32, (A_CHUNK, A_CHUNK), 1)
    keep = (r // mix_block == c // mix_block) & (c % mix_block <= r % mix_block)
    va16 = va.astype(BF16)
    for g in range(A_GROUPS):
        wg = jnp.where(keep, mixw_ref[g], 0.0).astype(BF16)
        bg = mixb_ref[g]
        for ch in range(rows // A_CHUNK):
            rs = slice(ch * A_CHUNK, (ch + 1) * A_CHUNK)
            cs = slice(g * A_WIDTH, (g + 1) * A_WIDTH)
            sg = _dot(wg, va16[rs, cs]) + bg
            mixa_ref[rs, cs] = (u[rs, cs] * sg).astype(BF16)

    q0 = 2 * A_DIM
    q = _head_rms(proj(q0, B_DIM), qn_ref[...])
    q_ref[...] = (q * (B_HEAD_DIM ** -0.5)).astype(BF16)
    k_ref[...] = _head_rms(proj(q0 + B_DIM, B_DIM), kn_ref[...])
    v_ref[...] = proj(q0 + 2 * B_DIM, B_DIM)


def _in_proj(x, ln1, w_in16, avn, qn, kn, mixw, mixb, *, rows_per_step, mix_block, emit_va):
    n = x.shape[0]
    row = lambda width: pl.BlockSpec((rows_per_step, width), lambda i: (i, 0))
    out_shape = [jax.ShapeDtypeStruct((n, A_DIM), BF16), jax.ShapeDtypeStruct((n, B_DIM), BF16),
                 jax.ShapeDtypeStruct((n, B_DIM), F32), jax.ShapeDtypeStruct((n, B_DIM), F32)]
    out_specs = [row(A_DIM), row(B_DIM), row(B_DIM), row(B_DIM)]
    if emit_va:
        out_shape.append(jax.ShapeDtypeStruct((n, A_DIM), F32))
        out_specs.append(row(A_DIM))
    return pl.pallas_call(
        functools.partial(_in_proj_kernel, mix_block=mix_block, emit_va=emit_va),
        grid=(n // rows_per_step,),
        in_specs=[row(D_MODEL), _const_spec((1, D_MODEL)), _const_spec((D_MODEL, IN_DIM)),
                  _const_spec((1, A_DIM)), _const_spec((1, B_DIM)), _const_spec((1, B_DIM)),
                  _const_spec((A_GROUPS, A_CHUNK, A_CHUNK)), _const_spec((A_GROUPS, A_CHUNK, A_WIDTH))],
        out_specs=out_specs,
        out_shape=out_shape,
        compiler_params=pltpu.CompilerParams(dimension_semantics=("arbitrary",), vmem_limit_bytes=VMEM_LIMIT),
        name="in_proj_sample" if emit_va else "in_proj_prompt",
    )(x, ln1, w_in16, avn, qn, kn, mixw, mixb)


def _top_k_mask(g, valid, lane0, n_cand):
    lane = lax.broadcasted_iota(jnp.int32, g.shape, 1)
    gm = jnp.where(valid, g, NEG_INF)
    rank = jnp.zeros(g.shape, jnp.int32)
    for m in range(lane0, lane0 + n_cand):
        col = gm[:, m:m + 1]
        beats = (col > gm) | ((col == gm) & (lane > m))
        rank = rank + beats.astype(jnp.int32)
    return valid & (rank < MOBA_TOPK)


def _moba_prompt_kernel(q_ref, k_ref, v_ref, o_ref, kaug_ref, vaug_ref):
    seq = q_ref.shape[0]
    n_blocks = seq // MOBA_BLOCK
    heads = ((0, B_HEAD_DIM), (B_HEAD_DIM, 0))

    k = k_ref[...]
    v = v_ref[...]
    lane_s = lax.broadcasted_iota(jnp.int32, (seq, LANES), 1)
    blk_s = lax.broadcasted_iota(jnp.int32, (seq, LANES), 0) // MOBA_BLOCK
    kmean = jnp.concatenate(
        [jnp.mean(k_ref[j * MOBA_BLOCK:(j + 1) * MOBA_BLOCK, :], axis=0, keepdims=True) for j in range(n_blocks)],
        axis=0)
    kmean_pad = []
    for h, (d0, a0) in enumerate(heads):
        in_head = (lane_s >= d0) & (lane_s < d0 + B_HEAD_DIM)
        kaug_ref[h] = jnp.where(in_head, k, jnp.where(lane_s - a0 == blk_s, 1.0, 0.0)).astype(BF16)
        vaug_ref[h] = jnp.where(in_head, v, jnp.where(lane_s == a0, 1.0, 0.0)).astype(BF16)
        kmean_pad.append(jnp.concatenate(
            ([jnp.zeros((a0, LANES), F32)] if a0 else []) + [kmean, jnp.zeros((LANES - n_blocks - a0, LANES), F32)],
            axis=0).astype(BF16))

    lane_q = lax.broadcasted_iota(jnp.int32, (MOBA_BLOCK, LANES), 1)
    kpos = lax.broadcasted_iota(jnp.int32, (MOBA_BLOCK, MOBA_BLOCK), 1)
    qpos = lax.broadcasted_iota(jnp.int32, (MOBA_BLOCK, MOBA_BLOCK), 0)
    causal = kpos <= qpos

    for i in range(n_blocks):
        rows = slice(i * MOBA_BLOCK, (i + 1) * MOBA_BLOCK)
        n_keys = (i + 1) * MOBA_BLOCK
        q_pair = q_ref[rows, :]
        outs = []
        for h, (d0, a0) in enumerate(heads):
            in_head = (lane_q >= d0) & (lane_q < d0 + B_HEAD_DIM)
            q_aug = jnp.where(in_head, q_pair, jnp.zeros_like(q_pair))
            if i > MOBA_TOPK:
                past = (lane_q >= a0) & (lane_q < a0 + i)
                sel = _top_k_mask(_dot_nt(q_aug, kmean_pad[h]), past, a0, i)
                bias = jnp.where(past & ~sel, MASK_BIAS, 0.0).astype(BF16)
                q_aug = jnp.where(in_head, q_pair, bias)
            s = _dot_nt(q_aug, kaug_ref[h, 0:n_keys, :])
            s_own = jnp.where(causal, s[:, n_keys - MOBA_BLOCK:], MASK_BIAS)
            s = jnp.concatenate([s[:, :n_keys - MOBA_BLOCK], s_own], axis=-1) if i else s_own
            p = jnp.exp(s - jnp.max(s, axis=-1, keepdims=True)).astype(BF16)
            acc = _dot(p, vaug_ref[h, 0:n_keys, :])
            outs.append(acc / acc[:, a0:a0 + 1])
        o_ref[rows, :] = jnp.where(lane_q < B_HEAD_DIM, outs[0], outs[1]).astype(o_ref.dtype)


def _moba_prompt(q, k, v, *, batch, seq):
    spec = pl.BlockSpec((seq, LANES), lambda b, hp: (b, hp))
    return pl.pallas_call(
        _moba_prompt_kernel,
        grid=(batch, B_DIM // LANES),
        in_specs=[spec, spec, spec],
        out_specs=spec,
        out_shape=jax.ShapeDtypeStruct((batch * seq, B_DIM), BF16),
        scratch_shapes=[pltpu.VMEM((2, seq, LANES), BF16), pltpu.VMEM((2, seq, LANES), BF16)],
        compiler_params=pltpu.CompilerParams(dimension_semantics=("arbitrary", "arbitrary"),
                                             vmem_limit_bytes=VMEM_LIMIT),
        name="moba_prompt",
    )(q, k, v)


def _moba_sample_kernel(pt_ref, qrep_ref, knew_ref, vnew_ref, *refs, n_past_blocks, page):
    del pt_ref
    k_pages = refs[:PAGES_PER_STEP]
    v_pages = refs[PAGES_PER_STEP:2 * PAGES_PER_STEP]
    o_ref, s_ref, kv16_ref, acc_ref, l_ref = refs[2 * PAGES_PER_STEP:]
    j = pl.program_id(1)
    k_steps = n_past_blocks * MOBA_BLOCK // (page * PAGES_PER_STEP)
    n_rows = qrep_ref.shape[0]
    blocks_per_step = PAGES_PER_STEP * page // MOBA_BLOCK

    row = lax.broadcasted_iota(jnp.int32, (n_rows, B_DIM), 0)
    col = lax.broadcasted_iota(jnp.int32, (n_rows, B_DIM), 1)
    own_head = col // B_HEAD_DIM == row % B_HEADS
    q_heads = jnp.where(own_head, qrep_ref[...], jnp.zeros((n_rows, B_DIM), BF16))

    def stage_pages(pages):
        for r in range(PAGES_PER_STEP):
            kv16_ref[:, r * page:(r + 1) * page] = pages[r][...].astype(BF16)

    def block_cols(n):
        c0 = (n % blocks_per_step) * MOBA_BLOCK
        return n // blocks_per_step, slice(c0, c0 + MOBA_BLOCK)

    @pl.when(j < k_steps)
    def _scores():
        stage_pages(k_pages)
        s_ref[j] = _dot(q_heads, kv16_ref[...])

    @pl.when(j == k_steps)
    def _select():
        lane = lax.broadcasted_iota(jnp.int32, (n_rows, LANES), 1)
        qrow = lax.broadcasted_iota(jnp.int32, (n_rows, LANES), 0) // B_HEADS
        gate = jnp.zeros((n_rows, LANES), F32)
        for n in range(n_past_blocks):
            st, cols = block_cols(n)
            gate = jnp.where(lane == n, jnp.sum(s_ref[st, :, cols], axis=-1, keepdims=True), gate)
        sel = _top_k_mask(gate, lane < n_past_blocks, 0, n_past_blocks).astype(F32)

        k_own = jnp.concatenate([knew_ref[...], jnp.zeros((LANES - knew_ref.shape[0], B_DIM), F32)], axis=0)
        own_ok = lane <= qrow
        s_own = jnp.where(own_ok, _dot_nt(q_heads, k_own.astype(BF16)), NEG_INF)

        sel_cols = [sel[:, n:n + 1] > 0.0 for n in range(n_past_blocks)]
        m_run = jnp.full((n_rows, MOBA_BLOCK), NEG_INF, F32)
        for n in range(n_past_blocks):
            st, cols = block_cols(n)
            m_run = jnp.maximum(m_run, jnp.where(sel_cols[n], s_ref[st, :, cols], NEG_INF))
        m = jnp.maximum(jnp.max(m_run, axis=-1, keepdims=True), jnp.max(s_own, axis=-1, keepdims=True))

        p_own = jnp.exp(s_own - m)
        l_run = jnp.zeros((n_rows, MOBA_BLOCK), F32)
        for n in range(n_past_blocks):
            st, cols = block_cols(n)
            pn = jnp.where(sel_cols[n], jnp.exp(s_ref[st, :, cols] - m), 0.0)
            s_ref[st, :, cols] = pn
            l_run = l_run + pn
        l_ref[...] = jnp.broadcast_to(
            jnp.sum(l_run, axis=-1, keepdims=True) + jnp.sum(p_own, axis=-1, keepdims=True), l_ref.shape)
        acc = jnp.zeros((n_rows, B_DIM), F32)
        for t in range(vnew_ref.shape[0]):
            acc = acc + p_own[:, t:t + 1] * vnew_ref[t:t + 1, :]
        acc_ref[...] = acc

    @pl.when(j >= k_steps)
    def _values():
        stage_pages(v_pages)
        acc_ref[...] += _dot_nt(s_ref[j - k_steps].astype(BF16), kv16_ref[...])

    @pl.when(j == 2 * k_steps - 1)
    def _finish():
        out = jnp.where(own_head, acc_ref[...] / l_ref[:, 0:1], 0.0)
        n_q = n_rows // B_HEADS
        rows = [jnp.sum(out[t * B_HEADS:(t + 1) * B_HEADS, :], axis=0, keepdims=True) for t in range(n_q)]
        o_ref[...] = jnp.concatenate(rows + [jnp.zeros((o_ref.shape[0] - n_q, B_DIM), F32)], axis=0)


def _moba_sample(page_table, qrep, knew, vnew, cache_k, cache_v, *, n_new):
    dec_batch, n_pages = page_table.shape
    page = cache_k.shape[2]
    past_len = n_pages * page
    n_past_blocks = past_len // MOBA_BLOCK
    k_steps = n_pages // PAGES_PER_STEP
    n_rows = n_new * B_HEADS
    new_pad = knew.shape[1]

    def k_page(r):
        return pl.BlockSpec((None, B_DIM, page),
                            lambda b, j, pt: (pt[b, jnp.minimum(j, k_steps - 1) * PAGES_PER_STEP + r], 0, 0))

    def v_page(r):
        return pl.BlockSpec((None, B_DIM, page),
                            lambda b, j, pt: (pt[b, jnp.maximum(j - k_steps, 0) * PAGES_PER_STEP + r], 0, 0))

    per_batch = lambda rows: pl.BlockSpec((None, rows, B_DIM), lambda b, j, pt: (b, 0, 0))
    grid_spec = pltpu.PrefetchScalarGridSpec(
        num_scalar_prefetch=1,
        grid=(dec_batch, 2 * k_steps),
        in_specs=[per_batch(n_rows), per_batch(new_pad), per_batch(new_pad)]
        + [k_page(r) for r in range(PAGES_PER_STEP)] + [v_page(r) for r in range(PAGES_PER_STEP)],
        out_specs=per_batch(new_pad),
        scratch_shapes=[pltpu.VMEM((k_steps, n_rows, PAGES_PER_STEP * page), F32),
                        pltpu.VMEM((B_DIM, PAGES_PER_STEP * page), BF16),
                        pltpu.VMEM((n_rows, B_DIM), F32), pltpu.VMEM((n_rows, LANES), F32)],
    )
    return pl.pallas_call(
        functools.partial(_moba_sample_kernel, n_past_blocks=n_past_blocks, page=page),
        grid_spec=grid_spec,
        out_shape=jax.ShapeDtypeStruct((dec_batch, new_pad, B_DIM), F32),
        compiler_params=pltpu.CompilerParams(dimension_semantics=("arbitrary", "arbitrary"),
                                             vmem_limit_bytes=VMEM_LIMIT),
        name="moba_sample",
    )(page_table, qrep, knew, vnew, *([cache_k] * PAGES_PER_STEP), *([cache_v] * PAGES_PER_STEP))


def _mlp_ple_kernel(x_ref, ma_ref, mb_ref, p_ref, wout_ref, ln2_ref, wup_ref, wdn_ref, ln3_ref, wg_ref,
                    wpp_ref, pn_ref, y_ref, mix_ref, act_ref):
    mix_ref[:, 0:A_DIM] = ma_ref[...]
    mix_ref[:, A_DIM:A_DIM + B_DIM] = mb_ref[...]
    h = x_ref[...] + _dot(mix_ref[...], wout_ref[...])
    hn = _rms(h, ln2_ref[...]).astype(BF16)
    for c0 in range(0, D_FF, FF_CHUNK):
        act_ref[:, c0:c0 + FF_CHUNK] = jnp.square(jnp.maximum(_dot(hn, wup_ref[:, c0:c0 + FF_CHUNK]), 0.0)).astype(BF16)
    h = h + _dot(act_ref[...], wdn_ref[...])
    gate = 1.0 / (1.0 + jnp.exp(-_dot(_rms(h, ln3_ref[...]).astype(BF16), wg_ref[...])))
    e = _rms(_dot(p_ref[...].astype(BF16), wpp_ref[...]), pn_ref[...])
    y_ref[...] = h + gate * e


def _mlp_ple(x, mix_a, mix_b, p, w_out16, ln2, w_up16, w_down16, ln3, w_gate16, w_proj16, ple_norm, *,
             rows_per_step, name):
    n = x.shape[0]
    row = lambda width: pl.BlockSpec((rows_per_step, width), lambda i: (i, 0))
    return pl.pallas_call(
        _mlp_ple_kernel,
        grid=(n // rows_per_step,),
        in_specs=[row(D_MODEL), row(A_DIM), row(B_DIM), row(D_PLE),
                  _const_spec((A_DIM + B_DIM, D_MODEL)), _const_spec((1, D_MODEL)),
                  _const_spec((D_MODEL, D_FF)), _const_spec((D_FF, D_MODEL)), _const_spec((1, D_MODEL)),
                  _const_spec((D_MODEL, D_MODEL)), _const_spec((D_PLE, D_MODEL)), _const_spec((1, D_MODEL))],
        out_specs=row(D_MODEL),
        out_shape=jax.ShapeDtypeStruct((n, D_MODEL), F32),
        scratch_shapes=[pltpu.VMEM((rows_per_step, A_DIM + B_DIM), BF16), pltpu.VMEM((rows_per_step, D_FF), BF16)],
        compiler_params=pltpu.CompilerParams(dimension_semantics=("arbitrary",), vmem_limit_bytes=VMEM_LIMIT),
        name=name,
    )(x, mix_a, mix_b, p, w_out16, ln2, w_up16, w_down16, ln3, w_gate16, w_proj16, ple_norm)


def kernel(x_prompt, x_sample, p_prompt, p_sample, cache_k, cache_v, page_table, ln1, w_in, a_v_norm, a_ws, a_bs,
           q_norm, k_norm, w_out, ln2, w_up, w_down, ln3, w_ple_gate, w_ple_proj, ple_norm):
    depth = ln1.shape[0]
    batch, seq, _ = x_prompt.shape
    dec_batch, n_new, _ = x_sample.shape
    n_pool, page = cache_k.shape[1], cache_k.shape[2]
    n_p, n_s = batch * seq, dec_batch * n_new
    assert n_s == A_CHUNK and n_new <= A_CHUNK and A_CHUNK % n_new == 0 and seq % ROW_TILE == 0
    assert MOBA_BLOCK % page == 0 and (page_table.shape[1] * page) % MOBA_BLOCK == 0

    hp = x_prompt.reshape(n_p, D_MODEL)
    hs = x_sample.reshape(n_s, D_MODEL)
    kp_l, vp_l, ks_l, vs_l, as_l = [], [], [], [], []
    for i in range(depth):
        row = lambda a: a.reshape(1, -1)
        w_in16, w_out16, w_up16, w_down16 = (w[i].astype(BF16) for w in (w_in, w_out, w_up, w_down))
        w_gate16, w_proj16 = w_ple_gate[i].astype(BF16), w_ple_proj[i].astype(BF16)
        avn = row(a_v_norm[i])
        qn, kn = row(jnp.tile(q_norm[i], B_HEADS)), row(jnp.tile(k_norm[i], B_HEADS))
        reps = A_CHUNK // n_new
        bias_p = jnp.broadcast_to(a_bs[i][:, :, None], (A_GROUPS, A_CHUNK, A_WIDTH))
        mixw_s = jnp.tile(a_ws[i][:, :n_new, :n_new], (1, reps, reps))
        bias_s = jnp.broadcast_to(jnp.tile(a_bs[i][:, :n_new], (1, reps))[:, :, None], (A_GROUPS, A_CHUNK, A_WIDTH))

        mix_a, q, k, v = _in_proj(hp, row(ln1[i]), w_in16, avn, qn, kn, a_ws[i], bias_p,
                                  rows_per_step=ROW_TILE, mix_block=A_CHUNK, emit_va=False)
        mix_b = _moba_prompt(q, k, v, batch=batch, seq=seq)
        hp = _mlp_ple(hp, mix_a, mix_b, p_prompt[i].reshape(n_p, D_PLE), w_out16, row(ln2[i]), w_up16, w_down16,
                      row(ln3[i]), w_gate16, w_proj16, row(ple_norm[i]), rows_per_step=ROW_TILE, name="mlp_ple_prompt")
        kp_l.append(k.reshape(batch, seq, B_HEADS, B_HEAD_DIM))
        vp_l.append(v.reshape(batch, seq, B_HEADS, B_HEAD_DIM))

        mix_a, q, k, v, va = _in_proj(hs, row(ln1[i]), w_in16, avn, qn, kn, mixw_s, bias_s,
                                      rows_per_step=n_s, mix_block=n_new, emit_va=True)
        new_pad = 8
        pad_new = lambda a: jnp.pad(a.reshape(dec_batch, n_new, B_DIM), ((0, 0), (0, new_pad - n_new), (0, 0)))
        qrep = jnp.repeat(q.reshape(dec_batch, n_new, B_DIM), B_HEADS, axis=1)
        pages_t = lambda c: c.transpose(0, 2, 3, 1).reshape(n_pool, B_DIM, page)
        o = _moba_sample(page_table, qrep, pad_new(k), pad_new(v), pages_t(cache_k[i]), pages_t(cache_v[i]),
                         n_new=n_new)
        mix_b = o[:, :n_new, :].reshape(n_s, B_DIM).astype(BF16)
        hs = _mlp_ple(hs, mix_a, mix_b, p_sample[i].reshape(n_s, D_PLE), w_out16, row(ln2[i]), w_up16, w_down16,
                      row(ln3[i]), w_gate16, w_proj16, row(ple_norm[i]), rows_per_step=n_s, name="mlp_ple_sample")
        ks_l.append(k.reshape(dec_batch, n_new, B_HEADS, B_HEAD_DIM))
        vs_l.append(v.reshape(dec_batch, n_new, B_HEADS, B_HEAD_DIM))
        as_l.append(va.reshape(dec_batch, n_new, A_GROUPS, A_WIDTH))

    return (hp.reshape(batch, seq, D_MODEL), hs.reshape(dec_batch, n_new, D_MODEL),
            jnp.stack(kp_l), jnp.stack(vp_l), jnp.stack(ks_l), jnp.stack(vs_l), jnp.stack(as_l))
```

```python
import functools

import jax
import jax.numpy as jnp
from jax import lax
from jax.experimental import pallas as pl
from jax.experimental.pallas import tpu as pltpu

D_MODEL = 1024
D_PLE = 256
A_GROUPS = 4
A_WIDTH = 128
A_CHUNK = 128
A_DIM = A_GROUPS * A_WIDTH
B_HEADS = 8
B_HEAD_DIM = 64
B_DIM = B_HEADS * B_HEAD_DIM
MOBA_BLOCK = 256
MOBA_TOPK = 3
D_FF = 4 * D_MODEL
IN_DIM = 2 * A_DIM + 3 * B_DIM
EPS = 1e-6

LANES = 128
ROW_TILE = 512
FF_CHUNK = 1024
PAGES_PER_STEP = 16
VMEM_LIMIT = 56 * 1024 * 1024

BF16 = jnp.bfloat16
F32 = jnp.float32
NEG_INF = float("-inf")
MASK_BIAS = -1e30


def _dot(a, b):
    return jnp.dot(a, b, preferred_element_type=F32)


def _dot_nt(a, b):
    return lax.dot_general(a, b, (((1,), (1,)), ((), ())), preferred_element_type=F32)


def _rms(x, gain):
    return x * lax.rsqrt(jnp.mean(x * x, axis=-1, keepdims=True) + EPS) * gain


def _const_spec(shape):
    nd = len(shape)
    return pl.BlockSpec(shape, lambda *_: (0,) * nd, pipeline_mode=pl.Buffered(1))


def _head_rms(z, gain_row):
    lane = lax.broadcasted_iota(jnp.int32, (1, LANES), 1)
    lo = lane < B_HEAD_DIM
    outs = []
    for t in range(B_DIM // LANES):
        zt = z[:, t * LANES:(t + 1) * LANES]
        z2 = zt * zt
        s_lo = jnp.sum(jnp.where(lo, z2, 0.0), axis=-1, keepdims=True)
        s_hi = jnp.sum(jnp.where(lo, 0.0, z2), axis=-1, keepdims=True)
        ms = jnp.where(lo, s_lo, s_hi) * (1.0 / B_HEAD_DIM)
        outs.append(zt * lax.rsqrt(ms + EPS))
    return jnp.concatenate(outs, axis=-1) * gain_row


def _tile_corner(m, blk, tile_lanes):
    r = lax.broadcasted_iota(jnp.int32, m.shape, 0) % blk
    c = lax.broadcasted_iota(jnp.int32, m.shape, 1) % blk
    out = jnp.zeros_like(m)
    for t in range(blk):
        out = jnp.where(r == t, m[t:t + 1, :], out)
    if tile_lanes:
        by_rows, out = out, jnp.zeros_like(m)
        for t in range(blk):
            out = jnp.where(c == t, by_rows[:, t:t + 1], out)
    return out


def _in_proj_kernel(x_ref, ln1_ref, w_ref, avn_ref, qn_ref, kn_ref, mixw_ref, mixb_ref,
                    *out_refs, mix_block, prompt):
    if prompt:
        mixa_ref, q_ref, k_ref, v_ref = out_refs
    else:
        mixa_ref, q_ref, k_ref, v_ref, va_ref = out_refs
    rows = x_ref.shape[0]
    hn = _rms(x_ref[...], ln1_ref[...]).astype(BF16)

    def proj(c0, n):
        return _dot(hn, w_ref[:, c0:c0 + n])

    u = jax.nn.gelu(proj(0, A_DIM))
    gv = jax.nn.gelu(proj(A_DIM, A_DIM))
    va = jnp.concatenate(
        [_rms(gv[:, g * A_WIDTH:(g + 1) * A_WIDTH], 1.0) for g in range(A_GROUPS)], axis=-1) * avn_ref[...]
    if not prompt:
        va_ref[...] = va

    r = lax.broadcasted_iota(jnp.int32, (A_CHUNK, A_CHUNK), 0)
    c = lax.broadcasted_iota(jnp.int32, (A_CHUNK, A_CHUNK), 1)
    keep = (r // mix_block == c // mix_block) & (c % mix_block <= r % mix_block)
    va16 = va.astype(BF16)
    for g in range(A_GROUPS):
        wg, bg = mixw_ref[g], mixb_ref[g]
        if mix_block != A_CHUNK:
            wg, bg = _tile_corner(wg, mix_block, True), _tile_corner(bg, mix_block, False)
        wg = jnp.where(keep, wg, 0.0).astype(BF16)
        for ch in range(rows // A_CHUNK):
            rs = slice(ch * A_CHUNK, (ch + 1) * A_CHUNK)
            cs = slice(g * A_WIDTH, (g + 1) * A_WIDTH)
            sg = _dot(wg, va16[rs, cs]) + bg
            mixa_ref[rs, cs] = (u[rs, cs] * sg).astype(BF16)

    q0 = 2 * A_DIM
    q = _head_rms(proj(q0, B_DIM), qn_ref[...])
    q_ref[...] = (q * (B_HEAD_DIM ** -0.5)).astype(BF16)
    k = _head_rms(proj(q0 + B_DIM, B_DIM), kn_ref[...])
    v = proj(q0 + 2 * B_DIM, B_DIM)
    if prompt:
        k_ref[...] = k.T
        v_ref[...] = v.T
    else:
        k_ref[...] = k
        v_ref[...] = v


def _in_proj(x, ln1, w_in16, avn, qn, kn, mixw, mixb, *, rows_per_step, mix_block, seq=None):
    n = x.shape[0]
    prompt = seq is not None
    row = lambda width: pl.BlockSpec((rows_per_step, width), lambda i: (i, 0))
    out_shape = [jax.ShapeDtypeStruct((n, A_DIM), BF16), jax.ShapeDtypeStruct((n, B_DIM), BF16)]
    out_specs = [row(A_DIM), row(B_DIM)]
    if prompt:
        steps = seq // rows_per_step
        kv_t = pl.BlockSpec((None, B_DIM, rows_per_step), lambda i: (i // steps, 0, i % steps))
        out_shape += [jax.ShapeDtypeStruct((n // seq, B_DIM, seq), F32)] * 2
        out_specs += [kv_t, kv_t]
    else:
        out_shape += [jax.ShapeDtypeStruct((n, B_DIM), F32)] * 2 + [jax.ShapeDtypeStruct((n, A_DIM), F32)]
        out_specs += [row(B_DIM), row(B_DIM), row(A_DIM)]
    return pl.pallas_call(
        functools.partial(_in_proj_kernel, mix_block=mix_block, prompt=prompt),
        grid=(n // rows_per_step,),
        in_specs=[row(D_MODEL), _const_spec((1, D_MODEL)), _const_spec((D_MODEL, IN_DIM)),
                  _const_spec((1, A_DIM)), _const_spec((1, B_DIM)), _const_spec((1, B_DIM)),
                  _const_spec((A_GROUPS, A_CHUNK, A_CHUNK)), _const_spec((A_GROUPS, A_CHUNK, A_WIDTH))],
        out_specs=out_specs,
        out_shape=out_shape,
        compiler_params=pltpu.CompilerParams(dimension_semantics=("arbitrary",), vmem_limit_bytes=VMEM_LIMIT),
        name="in_proj_prompt" if prompt else "in_proj_sample",
    )(x, ln1, w_in16, avn, qn, kn, mixw, mixb)


def _top_k_mask(g, valid, lane0, n_cand):
    lane = lax.broadcasted_iota(jnp.int32, g.shape, 1)
    gm = jnp.where(valid, g, NEG_INF)
    rank = jnp.zeros(g.shape, jnp.int32)
    for m in range(lane0, lane0 + n_cand):
        col = gm[:, m:m + 1]
        beats = (col > gm) | ((col == gm) & (lane > m))
        rank = rank + beats.astype(jnp.int32)
    return valid & (rank < MOBA_TOPK)


def _moba_prompt_kernel(q_ref, kt_ref, vt_ref, o_ref, kaug_ref, vaug_ref):
    seq = q_ref.shape[0]
    n_blocks = seq // MOBA_BLOCK
    heads = ((0, B_HEAD_DIM), (B_HEAD_DIM, 0))

    kt = kt_ref[...]
    vt = vt_ref[...]
    chan_s = lax.broadcasted_iota(jnp.int32, (LANES, seq), 0)
    blk_s = lax.broadcasted_iota(jnp.int32, (LANES, seq), 1) // MOBA_BLOCK
    lane_c = lax.broadcasted_iota(jnp.int32, (LANES, LANES), 1)
    kmean = [jnp.mean(kt_ref[:, j * MOBA_BLOCK:(j + 1) * MOBA_BLOCK], axis=1, keepdims=True)
             for j in range(n_blocks)]
    kmean_pad = []
    for h, (d0, a0) in enumerate(heads):
        in_head = (chan_s >= d0) & (chan_s < d0 + B_HEAD_DIM)
        kaug_ref[h] = jnp.where(in_head, kt, jnp.where(chan_s - a0 == blk_s, 1.0, 0.0)).astype(BF16)
        vaug_ref[h] = jnp.where(in_head, vt, jnp.where(chan_s == a0, 1.0, 0.0)).astype(BF16)
        w = jnp.zeros((LANES, LANES), F32)
        for n in range(n_blocks):
            w = jnp.where(lane_c == a0 + n, kmean[n], w)
        kmean_pad.append(w.astype(BF16))

    lane_q = lax.broadcasted_iota(jnp.int32, (MOBA_BLOCK, LANES), 1)
    kpos = lax.broadcasted_iota(jnp.int32, (MOBA_BLOCK, MOBA_BLOCK), 1)
    qpos = lax.broadcasted_iota(jnp.int32, (MOBA_BLOCK, MOBA_BLOCK), 0)
    causal = kpos <= qpos

    for i in range(n_blocks):
        rows = slice(i * MOBA_BLOCK, (i + 1) * MOBA_BLOCK)
        n_keys = (i + 1) * MOBA_BLOCK
        q_pair = q_ref[rows, :]
        outs = []
        for h, (d0, a0) in enumerate(heads):
            in_head = (lane_q >= d0) & (lane_q < d0 + B_HEAD_DIM)
            q_aug = jnp.where(in_head, q_pair, jnp.zeros_like(q_pair))
            if i > MOBA_TOPK:
                past = (lane_q >= a0) & (lane_q < a0 + i)
                sel = _top_k_mask(_dot(q_aug, kmean_pad[h]), past, a0, i)
                bias = jnp.where(past & ~sel, MASK_BIAS, 0.0).astype(BF16)
                q_aug = jnp.where(in_head, q_pair, bias)
            s = _dot(q_aug, kaug_ref[h, :, 0:n_keys])
            s_own = jnp.where(causal, s[:, n_keys - MOBA_BLOCK:], MASK_BIAS)
            s = jnp.concatenate([s[:, :n_keys - MOBA_BLOCK], s_own], axis=-1) if i else s_own
            p = jnp.exp(s - jnp.max(s, axis=-1, keepdims=True)).astype(BF16)
            acc = _dot_nt(p, vaug_ref[h, :, 0:n_keys])
            outs.append(acc / acc[:, a0:a0 + 1])
        o_ref[rows, :] = jnp.where(lane_q < B_HEAD_DIM, outs[0], outs[1]).astype(o_ref.dtype)


def _moba_prompt(q, kt, vt, *, batch, seq):
    spec = pl.BlockSpec((seq, LANES), lambda b, hp: (b, hp))
    spec_t = pl.BlockSpec((None, LANES, seq), lambda b, hp: (b, hp, 0))
    return pl.pallas_call(
        _moba_prompt_kernel,
        grid=(batch, B_DIM // LANES),
        in_specs=[spec, spec_t, spec_t],
        out_specs=spec,
        out_shape=jax.ShapeDtypeStruct((batch * seq, B_DIM), BF16),
        scratch_shapes=[pltpu.VMEM((2, LANES, seq), BF16), pltpu.VMEM((2, LANES, seq), BF16)],
        compiler_params=pltpu.CompilerParams(dimension_semantics=("arbitrary", "arbitrary"),
                                             vmem_limit_bytes=VMEM_LIMIT),
        name="moba_prompt",
    )(q, kt, vt)


def _moba_sample_kernel(pt_ref, qrep_ref, knew_ref, vnew_ref, *refs, n_past_blocks, page):
    del pt_ref
    k_pages = refs[:PAGES_PER_STEP]
    v_pages = refs[PAGES_PER_STEP:2 * PAGES_PER_STEP]
    o_ref, s_ref, acc_ref, l_ref = refs[2 * PAGES_PER_STEP:]
    j = pl.program_id(1)
    k_steps = n_past_blocks * MOBA_BLOCK // (page * PAGES_PER_STEP)
    n_rows = qrep_ref.shape[0]
    blocks_per_step = PAGES_PER_STEP * page // MOBA_BLOCK

    row = lax.broadcasted_iota(jnp.int32, (n_rows, B_DIM), 0)
    col = lax.broadcasted_iota(jnp.int32, (n_rows, B_DIM), 1)
    own_head = col // B_HEAD_DIM == row % B_HEADS
    q_heads = jnp.where(own_head, qrep_ref[...].astype(F32), 0.0)

    def block_cols(n):
        c0 = (n % blocks_per_step) * MOBA_BLOCK
        return n // blocks_per_step, slice(c0, c0 + MOBA_BLOCK)

    @pl.when(j < k_steps)
    def _scores():
        for r in range(PAGES_PER_STEP):
            s_ref[j, :, r * page:(r + 1) * page] = _dot(q_heads, k_pages[r][...])

    @pl.when(j == k_steps)
    def _select():
        lane = lax.broadcasted_iota(jnp.int32, (n_rows, LANES), 1)
        qrow = lax.broadcasted_iota(jnp.int32, (n_rows, LANES), 0) // B_HEADS
        gate = jnp.zeros((n_rows, LANES), F32)
        for n in range(n_past_blocks):
            st, cols = block_cols(n)
            gate = jnp.where(lane == n, jnp.sum(s_ref[st, :, cols], axis=-1, keepdims=True), gate)
        sel = _top_k_mask(gate, lane < n_past_blocks, 0, n_past_blocks).astype(F32)

        k_own = jnp.concatenate([knew_ref[...], jnp.zeros((LANES - knew_ref.shape[0], B_DIM), F32)], axis=0)
        own_ok = lane <= qrow
        s_own = jnp.where(own_ok, _dot_nt(q_heads, k_own), NEG_INF)

        sel_cols = [sel[:, n:n + 1] > 0.0 for n in range(n_past_blocks)]
        m_run = jnp.full((n_rows, MOBA_BLOCK), NEG_INF, F32)
        for n in range(n_past_blocks):
            st, cols = block_cols(n)
            m_run = jnp.maximum(m_run, jnp.where(sel_cols[n], s_ref[st, :, cols], NEG_INF))
        m = jnp.maximum(jnp.max(m_run, axis=-1, keepdims=True), jnp.max(s_own, axis=-1, keepdims=True))

        p_own = jnp.exp(s_own - m)
        l_run = jnp.zeros((n_rows, MOBA_BLOCK), F32)
        for n in range(n_past_blocks):
            st, cols = block_cols(n)
            pn = jnp.where(sel_cols[n], jnp.exp(s_ref[st, :, cols] - m), 0.0)
            s_ref[st, :, cols] = pn
            l_run = l_run + pn
        l_ref[...] = jnp.broadcast_to(
            jnp.sum(l_run, axis=-1, keepdims=True) + jnp.sum(p_own, axis=-1, keepdims=True), l_ref.shape)
        acc = jnp.zeros((n_rows, B_DIM), F32)
        for t in range(vnew_ref.shape[0]):
            acc = acc + p_own[:, t:t + 1] * vnew_ref[t:t + 1, :]
        acc_ref[...] = acc

    @pl.when(j >= k_steps)
    def _values():
        pv = None
        for r in range(PAGES_PER_STEP):
            d = _dot_nt(s_ref[j - k_steps, :, r * page:(r + 1) * page], v_pages[r][...])
            pv = d if pv is None else pv + d
        acc_ref[...] += pv

    @pl.when(j == 2 * k_steps - 1)
    def _finish():
        out = jnp.where(own_head, acc_ref[...] / l_ref[:, 0:1], 0.0)
        n_q = n_rows // B_HEADS
        rows = [jnp.sum(out[t * B_HEADS:(t + 1) * B_HEADS, :], axis=0, keepdims=True) for t in range(n_q)]
        o_ref[...] = jnp.concatenate(rows + [jnp.zeros((o_ref.shape[0] - n_q, B_DIM), F32)], axis=0)


def _moba_sample(page_table, qrep, knew, vnew, cache_k, cache_v, *, n_new):
    dec_batch, n_pages = page_table.shape
    page = cache_k.shape[2]
    past_len = n_pages * page
    n_past_blocks = past_len // MOBA_BLOCK
    k_steps = n_pages // PAGES_PER_STEP
    n_rows = n_new * B_HEADS
    new_pad = knew.shape[1]

    def k_page(r):
        return pl.BlockSpec((None, B_DIM, page),
                            lambda b, j, pt: (pt[b, jnp.minimum(j, k_steps - 1) * PAGES_PER_STEP + r], 0, 0))

    def v_page(r):
        return pl.BlockSpec((None, B_DIM, page),
                            lambda b, j, pt: (pt[b, jnp.maximum(j - k_steps, 0) * PAGES_PER_STEP + r], 0, 0))

    per_batch = lambda rows: pl.BlockSpec((None, rows, B_DIM), lambda b, j, pt: (b, 0, 0))
    grid_spec = pltpu.PrefetchScalarGridSpec(
        num_scalar_prefetch=1,
        grid=(dec_batch, 2 * k_steps),
        in_specs=[per_batch(n_rows), per_batch(new_pad), per_batch(new_pad)]
        + [k_page(r) for r in range(PAGES_PER_STEP)] + [v_page(r) for r in range(PAGES_PER_STEP)],
        out_specs=per_batch(new_pad),
        scratch_shapes=[pltpu.VMEM((k_steps, n_rows, PAGES_PER_STEP * page), F32),
                        pltpu.VMEM((n_rows, B_DIM), F32), pltpu.VMEM((n_rows, LANES), F32)],
    )
    return pl.pallas_call(
        functools.partial(_moba_sample_kernel, n_past_blocks=n_past_blocks, page=page),
        grid_spec=grid_spec,
        out_shape=jax.ShapeDtypeStruct((dec_batch, new_pad, B_DIM), F32),
        compiler_params=pltpu.CompilerParams(dimension_semantics=("arbitrary", "arbitrary"),
                                             vmem_limit_bytes=VMEM_LIMIT),
        name="moba_sample",
    )(page_table, qrep, knew, vnew, *([cache_k] * PAGES_PER_STEP), *([cache_v] * PAGES_PER_STEP))


def _mlp_ple_kernel(x_ref, ma_ref, mb_ref, p_ref, wout_ref, ln2_ref, wup_ref, wdn_ref, ln3_ref, wg_ref,
                    wpp_ref, pn_ref, y_ref, mix_ref, act_ref):
    mix_ref[:, 0:A_DIM] = ma_ref[...]
    mix_ref[:, A_DIM:A_DIM + B_DIM] = mb_ref[...]
    h = x_ref[...] + _dot(mix_ref[...], wout_ref[...])
    hn = _rms(h, ln2_ref[...]).astype(BF16)
    for c0 in range(0, D_FF, FF_CHUNK):
        act_ref[:, c0:c0 + FF_CHUNK] = jnp.square(jnp.maximum(_dot(hn, wup_ref[:, c0:c0 + FF_CHUNK]), 0.0)).astype(BF16)
    h = h + _dot(act_ref[...], wdn_ref[...])
    gate = 1.0 / (1.0 + jnp.exp(-_dot(_rms(h, ln3_ref[...]).astype(BF16), wg_ref[...])))
    e = _rms(_dot(p_ref[...].astype(BF16), wpp_ref[...]), pn_ref[...])
    y_ref[...] = h + gate * e


def _mlp_ple(x, mix_a, mix_b, p, w_out16, ln2, w_up16, w_down16, ln3, w_gate16, w_proj16, ple_norm, *,
             rows_per_step, name):
    n = x.shape[0]
    row = lambda width: pl.BlockSpec((rows_per_step, width), lambda i: (i, 0))
    return pl.pallas_call(
        _mlp_ple_kernel,
        grid=(n // rows_per_step,),
        in_specs=[row(D_MODEL), row(A_DIM), row(B_DIM), row(D_PLE),
                  _const_spec((A_DIM + B_DIM, D_MODEL)), _const_spec((1, D_MODEL)),
                  _const_spec((D_MODEL, D_FF)), _const_spec((D_FF, D_MODEL)), _const_spec((1, D_MODEL)),
                  _const_spec((D_MODEL, D_MODEL)), _const_spec((D_PLE, D_MODEL)), _const_spec((1, D_MODEL))],
        out_specs=row(D_MODEL),
        out_shape=jax.ShapeDtypeStruct((n, D_MODEL), F32),
        scratch_shapes=[pltpu.VMEM((rows_per_step, A_DIM + B_DIM), BF16), pltpu.VMEM((rows_per_step, D_FF), BF16)],
        compiler_params=pltpu.CompilerParams(dimension_semantics=("arbitrary",), vmem_limit_bytes=VMEM_LIMIT),
        name=name,
    )(x, mix_a, mix_b, p, w_out16, ln2, w_up16, w_down16, ln3, w_gate16, w_proj16, ple_norm)


def kernel(x_prompt, x_sample, p_prompt, p_sample, cache_k, cache_v, page_table, ln1, w_in, a_v_norm, a_ws, a_bs,
           q_norm, k_norm, w_out, ln2, w_up, w_down, ln3, w_ple_gate, w_ple_proj, ple_norm):
    depth = ln1.shape[0]
    batch, seq, _ = x_prompt.shape
    dec_batch, n_new, _ = x_sample.shape
    n_pool, page = cache_k.shape[1], cache_k.shape[2]
    n_p, n_s = batch * seq, dec_batch * n_new
    assert n_s == A_CHUNK and n_new <= A_CHUNK and A_CHUNK % n_new == 0 and seq % ROW_TILE == 0
    assert MOBA_BLOCK % page == 0 and (page_table.shape[1] * page) % MOBA_BLOCK == 0

    hp = x_prompt.reshape(n_p, D_MODEL)
    hs = x_sample.reshape(n_s, D_MODEL)
    kp_l, vp_l, ks_l, vs_l, as_l = [], [], [], [], []
    for i in range(depth):
        row = lambda a: a.reshape(1, -1)
        w_in16, w_out16, w_up16, w_down16 = (w[i].astype(BF16) for w in (w_in, w_out, w_up, w_down))
        w_gate16, w_proj16 = w_ple_gate[i].astype(BF16), w_ple_proj[i].astype(BF16)
        avn = row(a_v_norm[i])
        qn, kn = row(jnp.tile(q_norm[i], B_HEADS)), row(jnp.tile(k_norm[i], B_HEADS))
        mix_bias = jnp.broadcast_to(a_bs[i][:, :, None], (A_GROUPS, A_CHUNK, A_WIDTH))

        mix_a, q, kt, vt = _in_proj(hp, row(ln1[i]), w_in16, avn, qn, kn, a_ws[i], mix_bias,
                                    rows_per_step=ROW_TILE, mix_block=A_CHUNK, seq=seq)
        mix_b = _moba_prompt(q, kt, vt, batch=batch, seq=seq)
        hp = _mlp_ple(hp, mix_a, mix_b, p_prompt[i].reshape(n_p, D_PLE), w_out16, row(ln2[i]), w_up16, w_down16,
                      row(ln3[i]), w_gate16, w_proj16, row(ple_norm[i]), rows_per_step=ROW_TILE, name="mlp_ple_prompt")
        heads_last = lambda t: t.reshape(batch, B_HEADS, B_HEAD_DIM, seq).transpose(0, 3, 1, 2)
        kp_l.append(heads_last(kt))
        vp_l.append(heads_last(vt))

        mix_a, q, k, v, va = _in_proj(hs, row(ln1[i]), w_in16, avn, qn, kn, a_ws[i], mix_bias,
                                      rows_per_step=n_s, mix_block=n_new)
        new_pad = 8
        pad_new = lambda a: jnp.pad(a.reshape(dec_batch, n_new, B_DIM), ((0, 0), (0, new_pad - n_new), (0, 0)))
        qrep = jnp.repeat(q.reshape(dec_batch, n_new, B_DIM), B_HEADS, axis=1)
        pages_t = lambda c: c.transpose(0, 2, 3, 1).reshape(n_pool, B_DIM, page)
        o = _moba_sample(page_table, qrep, pad_new(k), pad_new(v), pages_t(cache_k[i]), pages_t(cache_v[i]),
                         n_new=n_new)
        mix_b = o[:, :n_new, :].reshape(n_s, B_DIM).astype(BF16)
        hs = _mlp_ple(hs, mix_a, mix_b, p_sample[i].reshape(n_s, D_PLE), w_out16, row(ln2[i]), w_up16, w_down16,
                      row(ln3[i]), w_gate16, w_proj16, row(ple_norm[i]), rows_per_step=n_s, name="mlp_ple_sample")
        ks_l.append(k.reshape(dec_batch, n_new, B_HEADS, B_HEAD_DIM))
        vs_l.append(v.reshape(dec_batch, n_new, B_HEADS, B_HEAD_DIM))
        as_l.append(va.reshape(dec_batch, n_new, A_GROUPS, A_WIDTH))

    return (hp.reshape(batch, seq, D_MODEL), hs.reshape(dec_batch, n_new, D_MODEL),
            jnp.stack(kp_l), jnp.stack(vp_l), jnp.stack(ks_l), jnp.stack(vs_l), jnp.stack(as_l))
```

```python
import functools

import jax
import jax.numpy as jnp
from jax import lax
from jax.experimental import pallas as pl
from jax.experimental.pallas import tpu as pltpu

D_MODEL = 1024
D_PLE = 256
A_GROUPS = 4
A_WIDTH = 128
A_CHUNK = 128
A_DIM = A_GROUPS * A_WIDTH
B_HEADS = 8
B_HEAD_DIM = 64
B_DIM = B_HEADS * B_HEAD_DIM
MOBA_BLOCK = 256
MOBA_TOPK = 3
D_FF = 4 * D_MODEL
IN_DIM = 2 * A_DIM + 3 * B_DIM
EPS = 1e-6

LANES = 128
ROW_TILE = 512
FF_CHUNK = 1024
SAMPLE_PAGE_GROUP = 8
VMEM_LIMIT = 56 * 1024 * 1024

BF16 = jnp.bfloat16
F32 = jnp.float32
NEG_INF = float("-inf")
MASK_BIAS = -1e30


def _dot(a, b):
    return jnp.dot(a, b, preferred_element_type=F32)


def _dot_nt(a, b):
    return lax.dot_general(a, b, (((1,), (1,)), ((), ())), preferred_element_type=F32)


def _rms(x, gain):
    return x * lax.rsqrt(jnp.mean(x * x, axis=-1, keepdims=True) + EPS) * gain


def _const_spec(shape):
    nd = len(shape)
    return pl.BlockSpec(shape, lambda *_: (0,) * nd, pipeline_mode=pl.Buffered(1))


def _head_rms(z, gain_row):
    lane = lax.broadcasted_iota(jnp.int32, (1, LANES), 1)
    lo = lane < B_HEAD_DIM
    outs = []
    for t in range(B_DIM // LANES):
        zt = z[:, t * LANES:(t + 1) * LANES]
        z2 = zt * zt
        s_lo = jnp.sum(jnp.where(lo, z2, 0.0), axis=-1, keepdims=True)
        s_hi = jnp.sum(jnp.where(lo, 0.0, z2), axis=-1, keepdims=True)
        ms = jnp.where(lo, s_lo, s_hi) * (1.0 / B_HEAD_DIM)
        outs.append(zt * lax.rsqrt(ms + EPS))
    return jnp.concatenate(outs, axis=-1) * gain_row


def _tile_corner(m, blk, tile_lanes):
    r = lax.broadcasted_iota(jnp.int32, m.shape, 0) % blk
    c = lax.broadcasted_iota(jnp.int32, m.shape, 1) % blk
    out = jnp.zeros_like(m)
    for t in range(blk):
        out = jnp.where(r == t, m[t:t + 1, :], out)
    if tile_lanes:
        by_rows, out = out, jnp.zeros_like(m)
        for t in range(blk):
            out = jnp.where(c == t, by_rows[:, t:t + 1], out)
    return out


def _in_proj_kernel(x_ref, ln1_ref, w_ref, avn_ref, qn_ref, kn_ref, mixw_ref, mixb_ref,
                    *out_refs, mix_block, prompt):
    if prompt:
        mixa_ref, q_ref, k_ref, v_ref = out_refs
    else:
        mixa_ref, q_ref, k_ref, v_ref, va_ref = out_refs
    rows = x_ref.shape[0]
    hn = _rms(x_ref[...], ln1_ref[...]).astype(BF16)

    def proj(c0, n):
        return _dot(hn, w_ref[:, c0:c0 + n])

    u = jax.nn.gelu(proj(0, A_DIM))
    gv = jax.nn.gelu(proj(A_DIM, A_DIM))
    va = jnp.concatenate(
        [_rms(gv[:, g * A_WIDTH:(g + 1) * A_WIDTH], 1.0) for g in range(A_GROUPS)], axis=-1) * avn_ref[...]
    if not prompt:
        va_ref[...] = va

    r = lax.broadcasted_iota(jnp.int32, (A_CHUNK, A_CHUNK), 0)
    c = lax.broadcasted_iota(jnp.int32, (A_CHUNK, A_CHUNK), 1)
    keep = (r // mix_block == c // mix_block) & (c % mix_block <= r % mix_block)
    va16 = va.astype(BF16)
    for g in range(A_GROUPS):
        wg, bg = mixw_ref[g], mixb_ref[g]
        if mix_block != A_CHUNK:
            wg, bg = _tile_corner(wg, mix_block, True), _tile_corner(bg, mix_block, False)
        wg = jnp.where(keep, wg, 0.0).astype(BF16)
        for ch in range(rows // A_CHUNK):
            rs = slice(ch * A_CHUNK, (ch + 1) * A_CHUNK)
            cs = slice(g * A_WIDTH, (g + 1) * A_WIDTH)
            sg = _dot(wg, va16[rs, cs]) + bg
            mixa_ref[rs, cs] = (u[rs, cs] * sg).astype(BF16)

    q0 = 2 * A_DIM
    q = _head_rms(proj(q0, B_DIM), qn_ref[...])
    q_ref[...] = (q * (B_HEAD_DIM ** -0.5)).astype(BF16)
    k = _head_rms(proj(q0 + B_DIM, B_DIM), kn_ref[...])
    v = proj(q0 + 2 * B_DIM, B_DIM)
    if prompt:
        k_ref[...] = k.T
        v_ref[...] = v.T
    else:
        k_ref[...] = k
        v_ref[...] = v


def _in_proj(x, ln1, w_in16, avn, qn, kn, mixw, mixb, *, rows_per_step, mix_block, seq=None):
    n = x.shape[0]
    prompt = seq is not None
    row = lambda width: pl.BlockSpec((rows_per_step, width), lambda i: (i, 0))
    out_shape = [jax.ShapeDtypeStruct((n, A_DIM), BF16), jax.ShapeDtypeStruct((n, B_DIM), BF16)]
    out_specs = [row(A_DIM), row(B_DIM)]
    if prompt:
        steps = seq // rows_per_step
        kv_t = pl.BlockSpec((None, B_DIM, rows_per_step), lambda i: (i // steps, 0, i % steps))
        out_shape += [jax.ShapeDtypeStruct((n // seq, B_DIM, seq), F32)] * 2
        out_specs += [kv_t, kv_t]
    else:
        out_shape += [jax.ShapeDtypeStruct((n, B_DIM), F32)] * 2 + [jax.ShapeDtypeStruct((n, A_DIM), F32)]
        out_specs += [row(B_DIM), row(B_DIM), row(A_DIM)]
    return pl.pallas_call(
        functools.partial(_in_proj_kernel, mix_block=mix_block, prompt=prompt),
        grid=(n // rows_per_step,),
        in_specs=[row(D_MODEL), _const_spec((1, D_MODEL)), _const_spec((D_MODEL, IN_DIM)),
                  _const_spec((1, A_DIM)), _const_spec((1, B_DIM)), _const_spec((1, B_DIM)),
                  _const_spec((A_GROUPS, A_CHUNK, A_CHUNK)), _const_spec((A_GROUPS, A_CHUNK, A_WIDTH))],
        out_specs=out_specs,
        out_shape=out_shape,
        compiler_params=pltpu.CompilerParams(dimension_semantics=("arbitrary",), vmem_limit_bytes=VMEM_LIMIT),
        name="in_proj_prompt" if prompt else "in_proj_sample",
    )(x, ln1, w_in16, avn, qn, kn, mixw, mixb)


def _top_k_mask(g, valid, lane0, n_cand):
    lane = lax.broadcasted_iota(jnp.int32, g.shape, 1)
    gm = jnp.where(valid, g, NEG_INF)
    rank = jnp.zeros(g.shape, jnp.int32)
    for m in range(lane0, lane0 + n_cand):
        col = gm[:, m:m + 1]
        beats = (col > gm) | ((col == gm) & (lane > m))
        rank = rank + beats.astype(jnp.int32)
    return valid & (rank < MOBA_TOPK)


def _moba_prompt_kernel(q_ref, kt_ref, vt_ref, o_ref, kaug_ref, vaug_ref):
    seq = q_ref.shape[0]
    n_blocks = seq // MOBA_BLOCK
    heads = ((0, B_HEAD_DIM), (B_HEAD_DIM, 0))

    kt = kt_ref[...]
    vt = vt_ref[...]
    chan_s = lax.broadcasted_iota(jnp.int32, (LANES, seq), 0)
    blk_s = lax.broadcasted_iota(jnp.int32, (LANES, seq), 1) // MOBA_BLOCK
    lane_c = lax.broadcasted_iota(jnp.int32, (LANES, LANES), 1)
    kmean = [jnp.mean(kt_ref[:, j * MOBA_BLOCK:(j + 1) * MOBA_BLOCK], axis=1, keepdims=True)
             for j in range(n_blocks)]
    kmean_pad = []
    for h, (d0, a0) in enumerate(heads):
        in_head = (chan_s >= d0) & (chan_s < d0 + B_HEAD_DIM)
        kaug_ref[h] = jnp.where(in_head, kt, jnp.where(chan_s - a0 == blk_s, 1.0, 0.0)).astype(BF16)
        vaug_ref[h] = jnp.where(in_head, vt, jnp.where(chan_s == a0, 1.0, 0.0)).astype(BF16)
        w = jnp.zeros((LANES, LANES), F32)
        for n in range(n_blocks):
            w = jnp.where(lane_c == a0 + n, kmean[n], w)
        kmean_pad.append(w.astype(BF16))

    lane_q = lax.broadcasted_iota(jnp.int32, (MOBA_BLOCK, LANES), 1)
    kpos = lax.broadcasted_iota(jnp.int32, (MOBA_BLOCK, MOBA_BLOCK), 1)
    qpos = lax.broadcasted_iota(jnp.int32, (MOBA_BLOCK, MOBA_BLOCK), 0)
    causal = kpos <= qpos

    for i in range(n_blocks):
        rows = slice(i * MOBA_BLOCK, (i + 1) * MOBA_BLOCK)
        n_keys = (i + 1) * MOBA_BLOCK
        q_pair = q_ref[rows, :]
        outs = []
        for h, (d0, a0) in enumerate(heads):
            in_head = (lane_q >= d0) & (lane_q < d0 + B_HEAD_DIM)
            q_aug = jnp.where(in_head, q_pair, jnp.zeros_like(q_pair))
            if i > MOBA_TOPK:
                past = (lane_q >= a0) & (lane_q < a0 + i)
                sel = _top_k_mask(_dot(q_aug, kmean_pad[h]), past, a0, i)
                bias = jnp.where(past & ~sel, MASK_BIAS, 0.0).astype(BF16)
                q_aug = jnp.where(in_head, q_pair, bias)
            s = _dot(q_aug, kaug_ref[h, :, 0:n_keys])
            s_own = jnp.where(causal, s[:, n_keys - MOBA_BLOCK:], MASK_BIAS)
            s = jnp.concatenate([s[:, :n_keys - MOBA_BLOCK], s_own], axis=-1) if i else s_own
            p = jnp.exp(s - jnp.max(s, axis=-1, keepdims=True)).astype(BF16)
            acc = _dot_nt(p, vaug_ref[h, :, 0:n_keys])
            outs.append(acc / acc[:, a0:a0 + 1])
        o_ref[rows, :] = jnp.where(lane_q < B_HEAD_DIM, outs[0], outs[1]).astype(o_ref.dtype)


def _moba_prompt(q, kt, vt, *, batch, seq):
    spec = pl.BlockSpec((seq, LANES), lambda b, hp: (b, hp))
    spec_t = pl.BlockSpec((None, LANES, seq), lambda b, hp: (b, hp, 0))
    return pl.pallas_call(
        _moba_prompt_kernel,
        grid=(batch, B_DIM // LANES),
        in_specs=[spec, spec_t, spec_t],
        out_specs=spec,
        out_shape=jax.ShapeDtypeStruct((batch * seq, B_DIM), BF16),
        scratch_shapes=[pltpu.VMEM((2, LANES, seq), BF16), pltpu.VMEM((2, LANES, seq), BF16)],
        compiler_params=pltpu.CompilerParams(dimension_semantics=("arbitrary", "arbitrary"),
                                             vmem_limit_bytes=VMEM_LIMIT),
        name="moba_prompt",
    )(q, kt, vt)


def _moba_sample_kernel(pt_ref, qrep_ref, knew_ref, vnew_ref, ck_ref, cv_ref, o_ref,
                        kbuf_ref, vbuf_ref, sem_ref, s_ref, *, n_past_blocks):
    b = pl.program_id(0)
    n_pages, _, page = kbuf_ref.shape
    pages_per_block = MOBA_BLOCK // page
    n_rows = qrep_ref.shape[0]

    def page_copy(cache_ref, buf_ref, sem_idx, seq_idx, p):
        return pltpu.make_async_copy(cache_ref.at[pt_ref[seq_idx, p]], buf_ref.at[p], sem_ref.at[sem_idx])

    def for_all_pages(fn):
        lax.fori_loop(0, n_pages, lambda p, c: (fn(p), c)[1], 0)

    @pl.when(b == 0)
    def _first_keys():
        for_all_pages(lambda p: page_copy(ck_ref, kbuf_ref, 0, 0, p).start())

    for_all_pages(lambda p: page_copy(cv_ref, vbuf_ref, 1, b, p).start())

    row = lax.broadcasted_iota(jnp.int32, (n_rows, B_DIM), 0)
    col = lax.broadcasted_iota(jnp.int32, (n_rows, B_DIM), 1)
    own_head = col // B_HEAD_DIM == row % B_HEADS
    q_heads = jnp.where(own_head, qrep_ref[...].astype(F32), 0.0)

    for_all_pages(lambda p: page_copy(ck_ref, kbuf_ref, 0, b, p).wait())

    def score_group(g, c):
        for r in range(SAMPLE_PAGE_GROUP):
            p = g * SAMPLE_PAGE_GROUP + r
            s_ref[p] = _dot(q_heads, kbuf_ref[p])
        return c

    lax.fori_loop(0, n_pages // SAMPLE_PAGE_GROUP, score_group, 0)

    @pl.when(b + 1 < pl.num_programs(0))
    def _next_keys():
        for_all_pages(lambda p: page_copy(ck_ref, kbuf_ref, 0, b + 1, p).start())

    def block_scores(n):
        return [s_ref[pg] for pg in range(n * pages_per_block, (n + 1) * pages_per_block)]

    lane = lax.broadcasted_iota(jnp.int32, (n_rows, LANES), 1)
    qrow = lax.broadcasted_iota(jnp.int32, (n_rows, LANES), 0) // B_HEADS
    gate = jnp.zeros((n_rows, LANES), F32)
    for n in range(n_past_blocks):
        gate = jnp.where(lane == n, jnp.sum(sum(block_scores(n)), axis=-1, keepdims=True), gate)
    sel = _top_k_mask(gate, lane < n_past_blocks, 0, n_past_blocks).astype(F32)

    k_own = jnp.concatenate([knew_ref[...], jnp.zeros((LANES - knew_ref.shape[0], B_DIM), F32)], axis=0)
    own_ok = lane <= qrow
    s_own = jnp.where(own_ok, _dot_nt(q_heads, k_own), NEG_INF)

    sel_cols = [sel[:, n:n + 1] > 0.0 for n in range(n_past_blocks)]
    m_run = jnp.full((n_rows, page), NEG_INF, F32)
    for n in range(n_past_blocks):
        for sn in block_scores(n):
            m_run = jnp.maximum(m_run, jnp.where(sel_cols[n], sn, NEG_INF))
    m = jnp.maximum(jnp.max(m_run, axis=-1, keepdims=True), jnp.max(s_own, axis=-1, keepdims=True))

    p_own = jnp.exp(s_own - m)
    l_run = jnp.zeros((n_rows, page), F32)
    for pg in range(n_pages):
        pn = jnp.where(sel_cols[pg // pages_per_block], jnp.exp(s_ref[pg] - m), 0.0)
        s_ref[pg] = pn
        l_run = l_run + pn
    l = jnp.sum(l_run, axis=-1, keepdims=True) + jnp.sum(p_own, axis=-1, keepdims=True)
    acc = jnp.zeros((n_rows, B_DIM), F32)
    for t in range(vnew_ref.shape[0]):
        acc = acc + p_own[:, t:t + 1] * vnew_ref[t:t + 1, :]

    for_all_pages(lambda p: page_copy(cv_ref, vbuf_ref, 1, b, p).wait())

    def value_group(g, acc):
        pv = None
        for r in range(SAMPLE_PAGE_GROUP):
            p = g * SAMPLE_PAGE_GROUP + r
            d = _dot_nt(s_ref[p], vbuf_ref[p])
            pv = d if pv is None else pv + d
        return acc + pv

    acc = lax.fori_loop(0, n_pages // SAMPLE_PAGE_GROUP, value_group, acc)

    out = jnp.where(own_head, acc / l, 0.0)
    n_q = n_rows // B_HEADS
    rows = [jnp.sum(out[t * B_HEADS:(t + 1) * B_HEADS, :], axis=0, keepdims=True) for t in range(n_q)]
    o_ref[...] = jnp.concatenate(rows + [jnp.zeros((o_ref.shape[0] - n_q, B_DIM), F32)], axis=0)


def _moba_sample(page_table, qrep, knew, vnew, cache_k, cache_v, *, n_new):
    dec_batch, n_pages = page_table.shape
    page = cache_k.shape[2]
    past_len = n_pages * page
    n_past_blocks = past_len // MOBA_BLOCK
    n_rows = n_new * B_HEADS
    new_pad = knew.shape[1]
    assert n_pages % SAMPLE_PAGE_GROUP == 0

    per_seq = lambda rows: pl.BlockSpec((None, rows, B_DIM), lambda b, pt: (b, 0, 0))
    in_hbm = pl.BlockSpec(memory_space=pl.ANY)
    page_buf = pltpu.VMEM((n_pages, B_DIM, page), F32)
    grid_spec = pltpu.PrefetchScalarGridSpec(
        num_scalar_prefetch=1,
        grid=(dec_batch,),
        in_specs=[per_seq(n_rows), per_seq(new_pad), per_seq(new_pad), in_hbm, in_hbm],
        out_specs=per_seq(new_pad),
        scratch_shapes=[page_buf, page_buf, pltpu.SemaphoreType.DMA((2,)), pltpu.VMEM((n_pages, n_rows, page), F32)],
    )
    return pl.pallas_call(
        functools.partial(_moba_sample_kernel, n_past_blocks=n_past_blocks),
        grid_spec=grid_spec,
        out_shape=jax.ShapeDtypeStruct((dec_batch, new_pad, B_DIM), F32),
        compiler_params=pltpu.CompilerParams(dimension_semantics=("arbitrary",), vmem_limit_bytes=VMEM_LIMIT),
        name="moba_sample",
    )(page_table, qrep, knew, vnew, cache_k, cache_v)


def _mlp_ple_kernel(x_ref, ma_ref, mb_ref, p_ref, wout_ref, ln2_ref, wup_ref, wdn_ref, ln3_ref, wg_ref,
                    wpp_ref, pn_ref, y_ref, mix_ref, act_ref):
    mix_ref[:, 0:A_DIM] = ma_ref[...]
    mix_ref[:, A_DIM:A_DIM + B_DIM] = mb_ref[...]
    h = x_ref[...] + _dot(mix_ref[...], wout_ref[...])
    hn = _rms(h, ln2_ref[...]).astype(BF16)
    for c0 in range(0, D_FF, FF_CHUNK):
        act_ref[:, c0:c0 + FF_CHUNK] = jnp.square(jnp.maximum(_dot(hn, wup_ref[:, c0:c0 + FF_CHUNK]), 0.0)).astype(BF16)
    h = h + _dot(act_ref[...], wdn_ref[...])
    gate = 1.0 / (1.0 + jnp.exp(-_dot(_rms(h, ln3_ref[...]).astype(BF16), wg_ref[...])))
    e = _rms(_dot(p_ref[...].astype(BF16), wpp_ref[...]), pn_ref[...])
    y_ref[...] = h + gate * e


def _mlp_ple(x, mix_a, mix_b, p, w_out16, ln2, w_up16, w_down16, ln3, w_gate16, w_proj16, ple_norm, *,
             rows_per_step, name):
    n = x.shape[0]
    row = lambda width: pl.BlockSpec((rows_per_step, width), lambda i: (i, 0))
    return pl.pallas_call(
        _mlp_ple_kernel,
        grid=(n // rows_per_step,),
        in_specs=[row(D_MODEL), row(A_DIM), row(B_DIM), row(D_PLE),
                  _const_spec((A_DIM + B_DIM, D_MODEL)), _const_spec((1, D_MODEL)),
                  _const_spec((D_MODEL, D_FF)), _const_spec((D_FF, D_MODEL)), _const_spec((1, D_MODEL)),
                  _const_spec((D_MODEL, D_MODEL)), _const_spec((D_PLE, D_MODEL)), _const_spec((1, D_MODEL))],
        out_specs=row(D_MODEL),
        out_shape=jax.ShapeDtypeStruct((n, D_MODEL), F32),
        scratch_shapes=[pltpu.VMEM((rows_per_step, A_DIM + B_DIM), BF16), pltpu.VMEM((rows_per_step, D_FF), BF16)],
        compiler_params=pltpu.CompilerParams(dimension_semantics=("arbitrary",), vmem_limit_bytes=VMEM_LIMIT),
        name=name,
    )(x, mix_a, mix_b, p, w_out16, ln2, w_up16, w_down16, ln3, w_gate16, w_proj16, ple_norm)


def kernel(x_prompt, x_sample, p_prompt, p_sample, cache_k, cache_v, page_table, ln1, w_in, a_v_norm, a_ws, a_bs,
           q_norm, k_norm, w_out, ln2, w_up, w_down, ln3, w_ple_gate, w_ple_proj, ple_norm):
    depth = ln1.shape[0]
    batch, seq, _ = x_prompt.shape
    dec_batch, n_new, _ = x_sample.shape
    n_pool, page = cache_k.shape[1], cache_k.shape[2]
    n_p, n_s = batch * seq, dec_batch * n_new
    assert n_s == A_CHUNK and n_new <= A_CHUNK and A_CHUNK % n_new == 0 and seq % ROW_TILE == 0
    assert MOBA_BLOCK % page == 0 and (page_table.shape[1] * page) % MOBA_BLOCK == 0

    hp = x_prompt.reshape(n_p, D_MODEL)
    hs = x_sample.reshape(n_s, D_MODEL)
    kp_l, vp_l, ks_l, vs_l, as_l = [], [], [], [], []
    for i in range(depth):
        row = lambda a: a.reshape(1, -1)
        w_in16, w_out16, w_up16, w_down16 = (w[i].astype(BF16) for w in (w_in, w_out, w_up, w_down))
        w_gate16, w_proj16 = w_ple_gate[i].astype(BF16), w_ple_proj[i].astype(BF16)
        avn = row(a_v_norm[i])
        qn, kn = row(jnp.tile(q_norm[i], B_HEADS)), row(jnp.tile(k_norm[i], B_HEADS))
        mix_bias = jnp.broadcast_to(a_bs[i][:, :, None], (A_GROUPS, A_CHUNK, A_WIDTH))

        mix_a, q, kt, vt = _in_proj(hp, row(ln1[i]), w_in16, avn, qn, kn, a_ws[i], mix_bias,
                                    rows_per_step=ROW_TILE, mix_block=A_CHUNK, seq=seq)
        mix_b = _moba_prompt(q, kt, vt, batch=batch, seq=seq)
        hp = _mlp_ple(hp, mix_a, mix_b, p_prompt[i].reshape(n_p, D_PLE), w_out16, row(ln2[i]), w_up16, w_down16,
                      row(ln3[i]), w_gate16, w_proj16, row(ple_norm[i]), rows_per_step=ROW_TILE, name="mlp_ple_prompt")
        heads_last = lambda t: t.reshape(batch, B_HEADS, B_HEAD_DIM, seq).transpose(0, 3, 1, 2)
        kp_l.append(heads_last(kt))
        vp_l.append(heads_last(vt))

        mix_a, q, k, v, va = _in_proj(hs, row(ln1[i]), w_in16, avn, qn, kn, a_ws[i], mix_bias,
                                      rows_per_step=n_s, mix_block=n_new)
        new_pad = 8
        pad_new = lambda a: jnp.pad(a.reshape(dec_batch, n_new, B_DIM), ((0, 0), (0, new_pad - n_new), (0, 0)))
        qrep = jnp.repeat(q.reshape(dec_batch, n_new, B_DIM), B_HEADS, axis=1)
        pages_t = lambda c: c.transpose(0, 2, 3, 1).reshape(n_pool, B_DIM, page)
        o = _moba_sample(page_table, qrep, pad_new(k), pad_new(v), pages_t(cache_k[i]), pages_t(cache_v[i]),
                         n_new=n_new)
        mix_b = o[:, :n_new, :].reshape(n_s, B_DIM).astype(BF16)
        hs = _mlp_ple(hs, mix_a, mix_b, p_sample[i].reshape(n_s, D_PLE), w_out16, row(ln2[i]), w_up16, w_down16,
                      row(ln3[i]), w_gate16, w_proj16, row(ple_norm[i]), rows_per_step=n_s, name="mlp_ple_sample")
        ks_l.append(k.reshape(dec_batch, n_new, B_HEADS, B_HEAD_DIM))
        vs_l.append(v.reshape(dec_batch, n_new, B_HEADS, B_HEAD_DIM))
        as_l.append(va.reshape(dec_batch, n_new, A_GROUPS, A_WIDTH))

    return (hp.reshape(batch, seq, D_MODEL), hs.reshape(dec_batch, n_new, D_MODEL),
            jnp.stack(kp_l), jnp.stack(vp_l), jnp.stack(ks_l), jnp.stack(vs_l), jnp.stack(as_l))
```

```python
import functools

import jax
import jax.numpy as jnp
from jax import lax
from jax.experimental import pallas as pl
from jax.experimental.pallas import tpu as pltpu

D_MODEL = 1024
D_PLE = 256
A_GROUPS = 4
A_WIDTH = 128
A_CHUNK = 128
A_DIM = A_GROUPS * A_WIDTH
B_HEADS = 8
B_HEAD_DIM = 64
B_DIM = B_HEADS * B_HEAD_DIM
MOBA_BLOCK = 256
MOBA_TOPK = 3
D_FF = 4 * D_MODEL
IN_DIM = 2 * A_DIM + 3 * B_DIM
EPS = 1e-6

LANES = 128
ROW_TILE = 512
FF_CHUNK = 1024
SAMPLE_PAGE_GROUP = 8
VMEM_LIMIT = 56 * 1024 * 1024

BF16 = jnp.bfloat16
F32 = jnp.float32
NEG_INF = float("-inf")
LOG2_E = 1.4426950408889634
MASK_BIAS = -1e30


def _dot(a, b):
    return jnp.dot(a, b, preferred_element_type=F32)


def _dot_nt(a, b):
    return lax.dot_general(a, b, (((1,), (1,)), ((), ())), preferred_element_type=F32)


def _rms(x, gain):
    return x * lax.rsqrt(jnp.mean(x * x, axis=-1, keepdims=True) + EPS) * gain


def _const_spec(shape):
    nd = len(shape)
    return pl.BlockSpec(shape, lambda *_: (0,) * nd, pipeline_mode=pl.Buffered(1))


def _head_rms(z, gain_row):
    lane = lax.broadcasted_iota(jnp.int32, (1, LANES), 1)
    lo = lane < B_HEAD_DIM
    outs = []
    for t in range(B_DIM // LANES):
        zt = z[:, t * LANES:(t + 1) * LANES]
        z2 = zt * zt
        s_lo = jnp.sum(jnp.where(lo, z2, 0.0), axis=-1, keepdims=True)
        s_hi = jnp.sum(jnp.where(lo, 0.0, z2), axis=-1, keepdims=True)
        ms = jnp.where(lo, s_lo, s_hi) * (1.0 / B_HEAD_DIM)
        outs.append(zt * lax.rsqrt(ms + EPS))
    return jnp.concatenate(outs, axis=-1) * gain_row


def _tile_corner(m, blk, tile_lanes):
    r = lax.broadcasted_iota(jnp.int32, m.shape, 0) % blk
    c = lax.broadcasted_iota(jnp.int32, m.shape, 1) % blk
    out = jnp.zeros_like(m)
    for t in range(blk):
        out = jnp.where(r == t, m[t:t + 1, :], out)
    if tile_lanes:
        by_rows, out = out, jnp.zeros_like(m)
        for t in range(blk):
            out = jnp.where(c == t, by_rows[:, t:t + 1], out)
    return out


def _in_proj_kernel(x_ref, ln1_ref, w_ref, avn_ref, qn_ref, kn_ref, mixw_ref, mixb_ref,
                    *out_refs, mix_block, prompt):
    if prompt:
        mixa_ref, q_ref, k_ref, v_ref = out_refs
    else:
        mixa_ref, q_ref, k_ref, v_ref, va_ref = out_refs
    rows = x_ref.shape[0]
    hn = _rms(x_ref[...], ln1_ref[...]).astype(BF16)

    def proj(c0, n):
        return _dot(hn, w_ref[:, c0:c0 + n])

    u = jax.nn.gelu(proj(0, A_DIM))
    gv = jax.nn.gelu(proj(A_DIM, A_DIM))
    va = jnp.concatenate(
        [_rms(gv[:, g * A_WIDTH:(g + 1) * A_WIDTH], 1.0) for g in range(A_GROUPS)], axis=-1) * avn_ref[...]
    if not prompt:
        va_ref[...] = va

    r = lax.broadcasted_iota(jnp.int32, (A_CHUNK, A_CHUNK), 0)
    c = lax.broadcasted_iota(jnp.int32, (A_CHUNK, A_CHUNK), 1)
    keep = (r // mix_block == c // mix_block) & (c % mix_block <= r % mix_block)
    va16 = va.astype(BF16)
    for g in range(A_GROUPS):
        wg, bg = mixw_ref[g], mixb_ref[g]
        if mix_block != A_CHUNK:
            wg, bg = _tile_corner(wg, mix_block, True), _tile_corner(bg, mix_block, False)
        wg = jnp.where(keep, wg, 0.0).astype(BF16)
        for ch in range(rows // A_CHUNK):
            rs = slice(ch * A_CHUNK, (ch + 1) * A_CHUNK)
            cs = slice(g * A_WIDTH, (g + 1) * A_WIDTH)
            sg = _dot(wg, va16[rs, cs]) + bg
            mixa_ref[rs, cs] = (u[rs, cs] * sg).astype(BF16)

    q0 = 2 * A_DIM
    q = _head_rms(proj(q0, B_DIM), qn_ref[...])
    q_ref[...] = (q * (B_HEAD_DIM ** -0.5 * LOG2_E)).astype(BF16)
    k = _head_rms(proj(q0 + B_DIM, B_DIM), kn_ref[...])
    v = proj(q0 + 2 * B_DIM, B_DIM)
    if prompt:
        k_ref[...] = k.T
        v_ref[...] = v.T
    else:
        k_ref[...] = k
        v_ref[...] = v


def _in_proj(x, ln1, w_in16, avn, qn, kn, mixw, mixb, *, rows_per_step, mix_block, seq=None):
    n = x.shape[0]
    prompt = seq is not None
    row = lambda width: pl.BlockSpec((rows_per_step, width), lambda i: (i, 0))
    out_shape = [jax.ShapeDtypeStruct((n, A_DIM), BF16), jax.ShapeDtypeStruct((n, B_DIM), BF16)]
    out_specs = [row(A_DIM), row(B_DIM)]
    if prompt:
        steps = seq // rows_per_step
        kv_t = pl.BlockSpec((None, B_DIM, rows_per_step), lambda i: (i // steps, 0, i % steps))
        out_shape += [jax.ShapeDtypeStruct((n // seq, B_DIM, seq), F32)] * 2
        out_specs += [kv_t, kv_t]
    else:
        out_shape += [jax.ShapeDtypeStruct((n, B_DIM), F32)] * 2 + [jax.ShapeDtypeStruct((n, A_DIM), F32)]
        out_specs += [row(B_DIM), row(B_DIM), row(A_DIM)]
    return pl.pallas_call(
        functools.partial(_in_proj_kernel, mix_block=mix_block, prompt=prompt),
        grid=(n // rows_per_step,),
        in_specs=[row(D_MODEL), _const_spec((1, D_MODEL)), _const_spec((D_MODEL, IN_DIM)),
                  _const_spec((1, A_DIM)), _const_spec((1, B_DIM)), _const_spec((1, B_DIM)),
                  _const_spec((A_GROUPS, A_CHUNK, A_CHUNK)), _const_spec((A_GROUPS, A_CHUNK, A_WIDTH))],
        out_specs=out_specs,
        out_shape=out_shape,
        compiler_params=pltpu.CompilerParams(dimension_semantics=("arbitrary",), vmem_limit_bytes=VMEM_LIMIT),
        name="in_proj_prompt" if prompt else "in_proj_sample",
    )(x, ln1, w_in16, avn, qn, kn, mixw, mixb)


def _top_k_mask(g, valid, lane0, n_cand):
    lane = lax.broadcasted_iota(jnp.int32, g.shape, 1)
    gm = jnp.where(valid, g, NEG_INF)
    rank = jnp.zeros(g.shape, jnp.int32)
    for m in range(lane0, lane0 + n_cand):
        col = gm[:, m:m + 1]
        beats = (col > gm) | ((col == gm) & (lane > m))
        rank = rank + beats.astype(jnp.int32)
    return valid & (rank < MOBA_TOPK)


def _top_k_mask_pair(g, valid, n_cand):
    lane = lax.broadcasted_iota(jnp.int32, g.shape, 1)
    upper = lane >= B_HEAD_DIM
    idx = lane % B_HEAD_DIM
    gm = jnp.where(valid, g, NEG_INF)
    rank = jnp.zeros(g.shape, jnp.int32)
    for m in range(n_cand):
        col = jnp.where(upper, gm[:, B_HEAD_DIM + m:B_HEAD_DIM + m + 1], gm[:, m:m + 1])
        beats = (col > gm) | ((col == gm) & (idx > m))
        rank = rank + beats.astype(jnp.int32)
    return valid & (rank < MOBA_TOPK)


def _moba_prompt_kernel(q_ref, kt_ref, vt_ref, o_ref, kaug_ref, vaug_ref):
    seq = q_ref.shape[0]
    n_blocks = seq // MOBA_BLOCK
    heads = ((0, B_HEAD_DIM), (B_HEAD_DIM, 0))

    kt = kt_ref[...]
    vt = vt_ref[...]
    chan_s = lax.broadcasted_iota(jnp.int32, (LANES, seq), 0)
    blk_s = lax.broadcasted_iota(jnp.int32, (LANES, seq), 1) // MOBA_BLOCK
    lane_c = lax.broadcasted_iota(jnp.int32, (LANES, LANES), 1)
    kmean = [jnp.mean(kt_ref[:, j * MOBA_BLOCK:(j + 1) * MOBA_BLOCK], axis=1, keepdims=True)
             for j in range(n_blocks)]
    chan_c = lax.broadcasted_iota(jnp.int32, (LANES, LANES), 0)
    gate_w = jnp.zeros((LANES, LANES), F32)
    for h, (d0, a0) in enumerate(heads):
        in_head = (chan_s >= d0) & (chan_s < d0 + B_HEAD_DIM)
        kaug_ref[h] = jnp.where(in_head, kt, jnp.where(chan_s - a0 == blk_s, 1.0, 0.0)).astype(BF16)
        vaug_ref[h] = jnp.where(in_head, vt, jnp.where(chan_s == a0, 1.0, 0.0)).astype(BF16)
        in_head_c = (chan_c >= d0) & (chan_c < d0 + B_HEAD_DIM)
        for n in range(n_blocks):
            gate_w = jnp.where((lane_c == a0 + n) & in_head_c, kmean[n], gate_w)
    gate_w = gate_w.astype(BF16)

    lane_q = lax.broadcasted_iota(jnp.int32, (MOBA_BLOCK, LANES), 1)
    kpos = lax.broadcasted_iota(jnp.int32, (MOBA_BLOCK, MOBA_BLOCK), 1)
    qpos = lax.broadcasted_iota(jnp.int32, (MOBA_BLOCK, MOBA_BLOCK), 0)
    causal = kpos <= qpos

    for i in range(n_blocks):
        rows = slice(i * MOBA_BLOCK, (i + 1) * MOBA_BLOCK)
        n_keys = (i + 1) * MOBA_BLOCK
        q_pair = q_ref[rows, :]
        bias = jnp.zeros_like(q_pair)
        if i > MOBA_TOPK:
            past = lane_q % B_HEAD_DIM < i
            sel = _top_k_mask_pair(_dot(q_pair, gate_w), past, i)
            bias = jnp.where(past & ~sel, MASK_BIAS, 0.0).astype(BF16)
        outs = []
        for h, (d0, a0) in enumerate(heads):
            in_head = (lane_q >= d0) & (lane_q < d0 + B_HEAD_DIM)
            q_aug = jnp.where(in_head, q_pair, bias)
            s = _dot(q_aug, kaug_ref[h, :, 0:n_keys])
            s_own = jnp.where(causal, s[:, n_keys - MOBA_BLOCK:], MASK_BIAS)
            s = jnp.concatenate([s[:, :n_keys - MOBA_BLOCK], s_own], axis=-1) if i else s_own
            p = jnp.exp2(s - jnp.max(s, axis=-1, keepdims=True)).astype(BF16)
            acc = _dot_nt(p, vaug_ref[h, :, 0:n_keys])
            outs.append(acc / acc[:, a0:a0 + 1])
        o_ref[rows, :] = jnp.where(lane_q < B_HEAD_DIM, outs[0], outs[1]).astype(o_ref.dtype)


def _moba_prompt(q, kt, vt, *, batch, seq):
    spec = pl.BlockSpec((seq, LANES), lambda b, hp: (b, hp))
    spec_t = pl.BlockSpec((None, LANES, seq), lambda b, hp: (b, hp, 0))
    return pl.pallas_call(
        _moba_prompt_kernel,
        grid=(batch, B_DIM // LANES),
        in_specs=[spec, spec_t, spec_t],
        out_specs=spec,
        out_shape=jax.ShapeDtypeStruct((batch * seq, B_DIM), BF16),
        scratch_shapes=[pltpu.VMEM((2, LANES, seq), BF16), pltpu.VMEM((2, LANES, seq), BF16)],
        compiler_params=pltpu.CompilerParams(dimension_semantics=("arbitrary", "arbitrary"),
                                             vmem_limit_bytes=VMEM_LIMIT),
        name="moba_prompt",
    )(q, kt, vt)


def _moba_sample_kernel(pt_ref, qrep_ref, knew_ref, vnew_ref, ck_ref, cv_ref, o_ref,
                        kbuf_ref, vbuf_ref, sem_ref, s_ref, *, n_past_blocks):
    b = pl.program_id(0)
    n_pages, _, page = kbuf_ref.shape
    pages_per_block = MOBA_BLOCK // page
    n_rows = qrep_ref.shape[0]

    def page_copy(cache_ref, buf_ref, sem_idx, seq_idx, p):
        return pltpu.make_async_copy(cache_ref.at[pt_ref[seq_idx, p]], buf_ref.at[p], sem_ref.at[sem_idx])

    def for_all_pages(fn):
        lax.fori_loop(0, n_pages, lambda p, c: (fn(p), c)[1], 0)

    @pl.when(b == 0)
    def _first_keys():
        for_all_pages(lambda p: page_copy(ck_ref, kbuf_ref, 0, 0, p).start())

    for_all_pages(lambda p: page_copy(cv_ref, vbuf_ref, 1, b, p).start())

    row = lax.broadcasted_iota(jnp.int32, (n_rows, B_DIM), 0)
    col = lax.broadcasted_iota(jnp.int32, (n_rows, B_DIM), 1)
    own_head = col // B_HEAD_DIM == row % B_HEADS
    q_heads = jnp.where(own_head, qrep_ref[...].astype(F32), 0.0)

    for_all_pages(lambda p: page_copy(ck_ref, kbuf_ref, 0, b, p).wait())

    def score_group(g, c):
        for r in range(SAMPLE_PAGE_GROUP):
            p = g * SAMPLE_PAGE_GROUP + r
            s_ref[p] = _dot(q_heads, kbuf_ref[p])
        return c

    lax.fori_loop(0, n_pages // SAMPLE_PAGE_GROUP, score_group, 0)

    @pl.when(b + 1 < pl.num_programs(0))
    def _next_keys():
        for_all_pages(lambda p: page_copy(ck_ref, kbuf_ref, 0, b + 1, p).start())

    def block_scores(n):
        return [s_ref[pg] for pg in range(n * pages_per_block, (n + 1) * pages_per_block)]

    lane = lax.broadcasted_iota(jnp.int32, (n_rows, LANES), 1)
    qrow = lax.broadcasted_iota(jnp.int32, (n_rows, LANES), 0) // B_HEADS
    gate = jnp.zeros((n_rows, LANES), F32)
    for n in range(n_past_blocks):
        gate = jnp.where(lane == n, jnp.sum(sum(block_scores(n)), axis=-1, keepdims=True), gate)
    sel = _top_k_mask(gate, lane < n_past_blocks, 0, n_past_blocks).astype(F32)

    k_own = jnp.concatenate([knew_ref[...], jnp.zeros((LANES - knew_ref.shape[0], B_DIM), F32)], axis=0)
    own_ok = lane <= qrow
    s_own = jnp.where(own_ok, _dot_nt(q_heads, k_own), NEG_INF)

    sel_cols = [sel[:, n:n + 1] > 0.0 for n in range(n_past_blocks)]
    m_run = jnp.full((n_rows, page), NEG_INF, F32)
    for n in range(n_past_blocks):
        for sn in block_scores(n):
            m_run = jnp.maximum(m_run, jnp.where(sel_cols[n], sn, NEG_INF))
    m = jnp.maximum(jnp.max(m_run, axis=-1, keepdims=True), jnp.max(s_own, axis=-1, keepdims=True))

    p_own = jnp.exp2(s_own - m)
    l_run = jnp.zeros((n_rows, page), F32)
    for pg in range(n_pages):
        pn = jnp.where(sel_cols[pg // pages_per_block], jnp.exp2(s_ref[pg] - m), 0.0)
        s_ref[pg] = pn
        l_run = l_run + pn
    l = jnp.sum(l_run, axis=-1, keepdims=True) + jnp.sum(p_own, axis=-1, keepdims=True)
    acc = jnp.zeros((n_rows, B_DIM), F32)
    for t in range(vnew_ref.shape[0]):
        acc = acc + p_own[:, t:t + 1] * vnew_ref[t:t + 1, :]

    for_all_pages(lambda p: page_copy(cv_ref, vbuf_ref, 1, b, p).wait())

    def value_group(g, acc):
        pv = None
        for r in range(SAMPLE_PAGE_GROUP):
            p = g * SAMPLE_PAGE_GROUP + r
            d = _dot_nt(s_ref[p], vbuf_ref[p])
            pv = d if pv is None else pv + d
        return acc + pv

    acc = lax.fori_loop(0, n_pages // SAMPLE_PAGE_GROUP, value_group, acc)

    out = jnp.where(own_head, acc / l, 0.0)
    n_q = n_rows // B_HEADS
    rows = [jnp.sum(out[t * B_HEADS:(t + 1) * B_HEADS, :], axis=0, keepdims=True) for t in range(n_q)]
    o_ref[...] = jnp.concatenate(rows + [jnp.zeros((o_ref.shape[0] - n_q, B_DIM), F32)], axis=0)


def _moba_sample(page_table, qrep, knew, vnew, cache_k, cache_v, *, n_new):
    dec_batch, n_pages = page_table.shape
    page = cache_k.shape[2]
    past_len = n_pages * page
    n_past_blocks = past_len // MOBA_BLOCK
    n_rows = n_new * B_HEADS
    new_pad = knew.shape[1]
    assert n_pages % SAMPLE_PAGE_GROUP == 0

    per_seq = lambda rows: pl.BlockSpec((None, rows, B_DIM), lambda b, pt: (b, 0, 0))
    in_hbm = pl.BlockSpec(memory_space=pl.ANY)
    page_buf = pltpu.VMEM((n_pages, B_DIM, page), F32)
    grid_spec = pltpu.PrefetchScalarGridSpec(
        num_scalar_prefetch=1,
        grid=(dec_batch,),
        in_specs=[per_seq(n_rows), per_seq(new_pad), per_seq(new_pad), in_hbm, in_hbm],
        out_specs=per_seq(new_pad),
        scratch_shapes=[page_buf, page_buf, pltpu.SemaphoreType.DMA((2,)), pltpu.VMEM((n_pages, n_rows, page), F32)],
    )
    return pl.pallas_call(
        functools.partial(_moba_sample_kernel, n_past_blocks=n_past_blocks),
        grid_spec=grid_spec,
        out_shape=jax.ShapeDtypeStruct((dec_batch, new_pad, B_DIM), F32),
        compiler_params=pltpu.CompilerParams(dimension_semantics=("arbitrary",), vmem_limit_bytes=VMEM_LIMIT),
        name="moba_sample",
    )(page_table, qrep, knew, vnew, cache_k, cache_v)


def _mlp_ple_kernel(x_ref, ma_ref, mb_ref, p_ref, wout_ref, ln2_ref, wup_ref, wdn_ref, ln3_ref, wg_ref,
                    wpp_ref, pn_ref, y_ref, mix_ref, act_ref):
    mix_ref[:, 0:A_DIM] = ma_ref[...]
    mix_ref[:, A_DIM:A_DIM + B_DIM] = mb_ref[...]
    h = x_ref[...] + _dot(mix_ref[...], wout_ref[...])
    hn = _rms(h, ln2_ref[...]).astype(BF16)
    for c0 in range(0, D_FF, FF_CHUNK):
        act_ref[:, c0:c0 + FF_CHUNK] = jnp.square(jnp.maximum(_dot(hn, wup_ref[:, c0:c0 + FF_CHUNK]), 0.0)).astype(BF16)
    h = h + _dot(act_ref[...], wdn_ref[...])
    gate = 1.0 / (1.0 + jnp.exp(-_dot(_rms(h, ln3_ref[...]).astype(BF16), wg_ref[...])))
    e = _rms(_dot(p_ref[...].astype(BF16), wpp_ref[...]), pn_ref[...])
    y_ref[...] = h + gate * e


def _mlp_ple(x, mix_a, mix_b, p, w_out16, ln2, w_up16, w_down16, ln3, w_gate16, w_proj16, ple_norm, *,
             rows_per_step, name):
    n = x.shape[0]
    row = lambda width: pl.BlockSpec((rows_per_step, width), lambda i: (i, 0))
    return pl.pallas_call(
        _mlp_ple_kernel,
        grid=(n // rows_per_step,),
        in_specs=[row(D_MODEL), row(A_DIM), row(B_DIM), row(D_PLE),
                  _const_spec((A_DIM + B_DIM, D_MODEL)), _const_spec((1, D_MODEL)),
                  _const_spec((D_MODEL, D_FF)), _const_spec((D_FF, D_MODEL)), _const_spec((1, D_MODEL)),
                  _const_spec((D_MODEL, D_MODEL)), _const_spec((D_PLE, D_MODEL)), _const_spec((1, D_MODEL))],
        out_specs=row(D_MODEL),
        out_shape=jax.ShapeDtypeStruct((n, D_MODEL), F32),
        scratch_shapes=[pltpu.VMEM((rows_per_step, A_DIM + B_DIM), BF16), pltpu.VMEM((rows_per_step, D_FF), BF16)],
        compiler_params=pltpu.CompilerParams(dimension_semantics=("arbitrary",), vmem_limit_bytes=VMEM_LIMIT),
        name=name,
    )(x, mix_a, mix_b, p, w_out16, ln2, w_up16, w_down16, ln3, w_gate16, w_proj16, ple_norm)


def kernel(x_prompt, x_sample, p_prompt, p_sample, cache_k, cache_v, page_table, ln1, w_in, a_v_norm, a_ws, a_bs,
           q_norm, k_norm, w_out, ln2, w_up, w_down, ln3, w_ple_gate, w_ple_proj, ple_norm):
    depth = ln1.shape[0]
    batch, seq, _ = x_prompt.shape
    dec_batch, n_new, _ = x_sample.shape
    n_pool, page = cache_k.shape[1], cache_k.shape[2]
    n_p, n_s = batch * seq, dec_batch * n_new
    assert n_s == A_CHUNK and n_new <= A_CHUNK and A_CHUNK % n_new == 0 and seq % ROW_TILE == 0
    assert MOBA_BLOCK % page == 0 and (page_table.shape[1] * page) % MOBA_BLOCK == 0

    hp = x_prompt.reshape(n_p, D_MODEL)
    hs = x_sample.reshape(n_s, D_MODEL)
    kp_l, vp_l, ks_l, vs_l, as_l = [], [], [], [], []
    for i in range(depth):
        row = lambda a: a.reshape(1, -1)
        w_in16, w_out16, w_up16, w_down16 = (w[i].astype(BF16) for w in (w_in, w_out, w_up, w_down))
        w_gate16, w_proj16 = w_ple_gate[i].astype(BF16), w_ple_proj[i].astype(BF16)
        avn = row(a_v_norm[i])
        qn, kn = row(jnp.tile(q_norm[i], B_HEADS)), row(jnp.tile(k_norm[i], B_HEADS))
        mix_bias = jnp.broadcast_to(a_bs[i][:, :, None], (A_GROUPS, A_CHUNK, A_WIDTH))

        mix_a, q, kt, vt = _in_proj(hp, row(ln1[i]), w_in16, avn, qn, kn, a_ws[i], mix_bias,
                                    rows_per_step=ROW_TILE, mix_block=A_CHUNK, seq=seq)
        mix_b = _moba_prompt(q, kt, vt, batch=batch, seq=seq)
        hp = _mlp_ple(hp, mix_a, mix_b, p_prompt[i].reshape(n_p, D_PLE), w_out16, row(ln2[i]), w_up16, w_down16,
                      row(ln3[i]), w_gate16, w_proj16, row(ple_norm[i]), rows_per_step=ROW_TILE, name="mlp_ple_prompt")
        heads_last = lambda t: t.reshape(batch, B_HEADS, B_HEAD_DIM, seq).transpose(0, 3, 1, 2)
        kp_l.append(heads_last(kt))
        vp_l.append(heads_last(vt))

        mix_a, q, k, v, va = _in_proj(hs, row(ln1[i]), w_in16, avn, qn, kn, a_ws[i], mix_bias,
                                      rows_per_step=n_s, mix_block=n_new)
        new_pad = 8
        pad_new = lambda a: jnp.pad(a.reshape(dec_batch, n_new, B_DIM), ((0, 0), (0, new_pad - n_new), (0, 0)))
        qrep = jnp.repeat(q.reshape(dec_batch, n_new, B_DIM), B_HEADS, axis=1)
        pages_t = lambda c: c.transpose(0, 2, 3, 1).reshape(n_pool, B_DIM, page)
        o = _moba_sample(page_table, qrep, pad_new(k), pad_new(v), pages_t(cache_k[i]), pages_t(cache_v[i]),
                         n_new=n_new)
        mix_b = o[:, :n_new, :].reshape(n_s, B_DIM).astype(BF16)
        hs = _mlp_ple(hs, mix_a, mix_b, p_sample[i].reshape(n_s, D_PLE), w_out16, row(ln2[i]), w_up16, w_down16,
                      row(ln3[i]), w_gate16, w_proj16, row(ple_norm[i]), rows_per_step=n_s, name="mlp_ple_sample")
        ks_l.append(k.reshape(dec_batch, n_new, B_HEADS, B_HEAD_DIM))
        vs_l.append(v.reshape(dec_batch, n_new, B_HEADS, B_HEAD_DIM))
        as_l.append(va.reshape(dec_batch, n_new, A_GROUPS, A_WIDTH))

    return (hp.reshape(batch, seq, D_MODEL), hs.reshape(dec_batch, n_new, D_MODEL),
            jnp.stack(kp_l), jnp.stack(vp_l), jnp.stack(ks_l), jnp.stack(vs_l), jnp.stack(as_l))
```

```python
import functools

import jax
import jax.numpy as jnp
from jax import lax
from jax.experimental import pallas as pl
from jax.experimental.pallas import tpu as pltpu

D_MODEL = 1024
D_PLE = 256
A_GROUPS = 4
A_WIDTH = 128
A_CHUNK = 128
A_DIM = A_GROUPS * A_WIDTH
B_HEADS = 8
B_HEAD_DIM = 64
B_DIM = B_HEADS * B_HEAD_DIM
MOBA_BLOCK = 256
MOBA_TOPK = 3
D_FF = 4 * D_MODEL
IN_DIM = 2 * A_DIM + 3 * B_DIM
EPS = 1e-6

LANES = 128
ROW_TILE = 512
FF_CHUNK = 1024
SAMPLE_PAGE_GROUP = 8
VMEM_LIMIT = 56 * 1024 * 1024

BF16 = jnp.bfloat16
F32 = jnp.float32
NEG_INF = float("-inf")
LOG2_E = 1.4426950408889634
MASK_BIAS = -1e30


def _dot(a, b):
    return jnp.dot(a, b, preferred_element_type=F32)


def _dot_nt(a, b):
    return lax.dot_general(a, b, (((1,), (1,)), ((), ())), preferred_element_type=F32)


def _rms(x, gain):
    return x * lax.rsqrt(jnp.mean(x * x, axis=-1, keepdims=True) + EPS) * gain


def _const_spec(shape):
    nd = len(shape)
    return pl.BlockSpec(shape, lambda *_: (0,) * nd, pipeline_mode=pl.Buffered(1))


def _head_rms(z, gain_row):
    lane = lax.broadcasted_iota(jnp.int32, (1, LANES), 1)
    lo = lane < B_HEAD_DIM
    outs = []
    for t in range(B_DIM // LANES):
        zt = z[:, t * LANES:(t + 1) * LANES]
        z2 = zt * zt
        s_lo = jnp.sum(jnp.where(lo, z2, 0.0), axis=-1, keepdims=True)
        s_hi = jnp.sum(jnp.where(lo, 0.0, z2), axis=-1, keepdims=True)
        ms = jnp.where(lo, s_lo, s_hi) * (1.0 / B_HEAD_DIM)
        outs.append(zt * lax.rsqrt(ms + EPS))
    return jnp.concatenate(outs, axis=-1) * gain_row


def _tile_corner(m, blk, tile_lanes):
    r = lax.broadcasted_iota(jnp.int32, m.shape, 0) % blk
    c = lax.broadcasted_iota(jnp.int32, m.shape, 1) % blk
    out = jnp.zeros_like(m)
    for t in range(blk):
        out = jnp.where(r == t, m[t:t + 1, :], out)
    if tile_lanes:
        by_rows, out = out, jnp.zeros_like(m)
        for t in range(blk):
            out = jnp.where(c == t, by_rows[:, t:t + 1], out)
    return out


def _in_proj_kernel(x_ref, ln1_ref, w_ref, avn_ref, qn_ref, kn_ref, mixw_ref, mixb_ref,
                    *out_refs, mix_block, prompt):
    if prompt:
        mixa_ref, q_ref, k_ref, v_ref = out_refs
    else:
        mixa_ref, q_ref, k_ref, v_ref, va_ref = out_refs
    rows = x_ref.shape[0]
    hn = _rms(x_ref[...], ln1_ref[...]).astype(BF16)

    def proj(c0, n):
        return _dot(hn, w_ref[:, c0:c0 + n])

    u = jax.nn.gelu(proj(0, A_DIM))
    gv = jax.nn.gelu(proj(A_DIM, A_DIM))
    va = jnp.concatenate(
        [_rms(gv[:, g * A_WIDTH:(g + 1) * A_WIDTH], 1.0) for g in range(A_GROUPS)], axis=-1) * avn_ref[...]
    if not prompt:
        va_ref[...] = va

    r = lax.broadcasted_iota(jnp.int32, (A_CHUNK, A_CHUNK), 0)
    c = lax.broadcasted_iota(jnp.int32, (A_CHUNK, A_CHUNK), 1)
    keep = (r // mix_block == c // mix_block) & (c % mix_block <= r % mix_block)
    va16 = va.astype(BF16)
    for g in range(A_GROUPS):
        wg, bg = mixw_ref[g], mixb_ref[g]
        if mix_block != A_CHUNK:
            wg, bg = _tile_corner(wg, mix_block, True), _tile_corner(bg, mix_block, False)
        wg = jnp.where(keep, wg, 0.0).astype(BF16)
        for ch in range(rows // A_CHUNK):
            rs = slice(ch * A_CHUNK, (ch + 1) * A_CHUNK)
            cs = slice(g * A_WIDTH, (g + 1) * A_WIDTH)
            sg = _dot(wg, va16[rs, cs]) + bg
            mixa_ref[rs, cs] = (u[rs, cs] * sg).astype(BF16)

    q0 = 2 * A_DIM
    q = _head_rms(proj(q0, B_DIM), qn_ref[...])
    q_ref[...] = (q * (B_HEAD_DIM ** -0.5 * LOG2_E)).astype(BF16)
    k = _head_rms(proj(q0 + B_DIM, B_DIM), kn_ref[...])
    v = proj(q0 + 2 * B_DIM, B_DIM)
    if prompt:
        k_ref[...] = k.T
        v_ref[...] = v.T
    else:
        k_ref[...] = k
        v_ref[...] = v


def _in_proj(x, ln1, w_in16, avn, qn, kn, mixw, mixb, *, rows_per_step, mix_block, seq=None):
    n = x.shape[0]
    prompt = seq is not None
    row = lambda width: pl.BlockSpec((rows_per_step, width), lambda i: (i, 0))
    out_shape = [jax.ShapeDtypeStruct((n, A_DIM), BF16), jax.ShapeDtypeStruct((n, B_DIM), BF16)]
    out_specs = [row(A_DIM), row(B_DIM)]
    if prompt:
        steps = seq // rows_per_step
        kv_t = pl.BlockSpec((None, B_DIM, rows_per_step), lambda i: (i // steps, 0, i % steps))
        out_shape += [jax.ShapeDtypeStruct((n // seq, B_DIM, seq), F32)] * 2
        out_specs += [kv_t, kv_t]
    else:
        out_shape += [jax.ShapeDtypeStruct((n, B_DIM), F32)] * 2 + [jax.ShapeDtypeStruct((n, A_DIM), F32)]
        out_specs += [row(B_DIM), row(B_DIM), row(A_DIM)]
    return pl.pallas_call(
        functools.partial(_in_proj_kernel, mix_block=mix_block, prompt=prompt),
        grid=(n // rows_per_step,),
        in_specs=[row(D_MODEL), _const_spec((1, D_MODEL)), _const_spec((D_MODEL, IN_DIM)),
                  _const_spec((1, A_DIM)), _const_spec((1, B_DIM)), _const_spec((1, B_DIM)),
                  _const_spec((A_GROUPS, A_CHUNK, A_CHUNK)), _const_spec((A_GROUPS, A_CHUNK, A_WIDTH))],
        out_specs=out_specs,
        out_shape=out_shape,
        compiler_params=pltpu.CompilerParams(dimension_semantics=("arbitrary",), vmem_limit_bytes=VMEM_LIMIT),
        name="in_proj_prompt" if prompt else "in_proj_sample",
    )(x, ln1, w_in16, avn, qn, kn, mixw, mixb)


def _top_k_mask(g, valid, lane0, n_cand):
    lane = lax.broadcasted_iota(jnp.int32, g.shape, 1)
    gm = jnp.where(valid, g, NEG_INF)
    rank = jnp.zeros(g.shape, jnp.int32)
    for m in range(lane0, lane0 + n_cand):
        col = gm[:, m:m + 1]
        beats = (col > gm) | ((col == gm) & (lane > m))
        rank = rank + beats.astype(jnp.int32)
    return valid & (rank < MOBA_TOPK)


def _top_k_mask_pair(g, valid, n_cand):
    lane = lax.broadcasted_iota(jnp.int32, g.shape, 1)
    upper = lane >= B_HEAD_DIM
    idx = lane % B_HEAD_DIM
    gm = jnp.where(valid, g, NEG_INF)
    rank = jnp.zeros(g.shape, jnp.int32)
    for m in range(n_cand):
        col = jnp.where(upper, gm[:, B_HEAD_DIM + m:B_HEAD_DIM + m + 1], gm[:, m:m + 1])
        beats = (col > gm) | ((col == gm) & (idx > m))
        rank = rank + beats.astype(jnp.int32)
    return valid & (rank < MOBA_TOPK)


def _moba_prompt_kernel(q_ref, kt_ref, vt_ref, o_ref, kaug_ref, vaug_ref):
    seq = q_ref.shape[0]
    n_blocks = seq // MOBA_BLOCK
    heads = ((0, B_HEAD_DIM), (B_HEAD_DIM, 0))

    kt = kt_ref[...]
    vt = vt_ref[...]
    chan_s = lax.broadcasted_iota(jnp.int32, (LANES, seq), 0)
    blk_s = lax.broadcasted_iota(jnp.int32, (LANES, seq), 1) // MOBA_BLOCK
    lane_c = lax.broadcasted_iota(jnp.int32, (LANES, LANES), 1)
    kmean = [jnp.mean(kt_ref[:, j * MOBA_BLOCK:(j + 1) * MOBA_BLOCK], axis=1, keepdims=True)
             for j in range(n_blocks)]
    chan_c = lax.broadcasted_iota(jnp.int32, (LANES, LANES), 0)
    gate_w = jnp.zeros((LANES, LANES), F32)
    for h, (d0, a0) in enumerate(heads):
        in_head = (chan_s >= d0) & (chan_s < d0 + B_HEAD_DIM)
        kaug_ref[h] = jnp.where(in_head, kt, jnp.where(chan_s - a0 == blk_s, 1.0, 0.0)).astype(BF16)
        vaug_ref[h] = jnp.where(in_head, vt, jnp.where(chan_s == a0, 1.0, 0.0)).astype(BF16)
        in_head_c = (chan_c >= d0) & (chan_c < d0 + B_HEAD_DIM)
        for n in range(n_blocks):
            gate_w = jnp.where((lane_c == a0 + n) & in_head_c, kmean[n], gate_w)
    gate_w = gate_w.astype(BF16)

    lane_q = lax.broadcasted_iota(jnp.int32, (MOBA_BLOCK, LANES), 1)
    kpos = lax.broadcasted_iota(jnp.int32, (MOBA_BLOCK, MOBA_BLOCK), 1)
    qpos = lax.broadcasted_iota(jnp.int32, (MOBA_BLOCK, MOBA_BLOCK), 0)
    causal = kpos <= qpos

    for i in range(n_blocks):
        rows = slice(i * MOBA_BLOCK, (i + 1) * MOBA_BLOCK)
        n_keys = (i + 1) * MOBA_BLOCK
        q_pair = q_ref[rows, :]
        bias = jnp.zeros_like(q_pair)
        if i > MOBA_TOPK:
            past = lane_q % B_HEAD_DIM < i
            sel = _top_k_mask_pair(_dot(q_pair, gate_w), past, i)
            bias = jnp.where(past & ~sel, MASK_BIAS, 0.0).astype(BF16)
        outs = []
        for h, (d0, a0) in enumerate(heads):
            in_head = (lane_q >= d0) & (lane_q < d0 + B_HEAD_DIM)
            q_aug = jnp.where(in_head, q_pair, bias)
            s = _dot(q_aug, kaug_ref[h, :, 0:n_keys])
            s_own = jnp.where(causal, s[:, n_keys - MOBA_BLOCK:], MASK_BIAS)
            s = jnp.concatenate([s[:, :n_keys - MOBA_BLOCK], s_own], axis=-1) if i else s_own
            p = jnp.exp2(s - jnp.max(s, axis=-1, keepdims=True)).astype(BF16)
            acc = _dot_nt(p, vaug_ref[h, :, 0:n_keys])
            outs.append(acc / acc[:, a0:a0 + 1])
        o_ref[rows, :] = jnp.where(lane_q < B_HEAD_DIM, outs[0], outs[1]).astype(o_ref.dtype)


def _moba_sample_kernel(pt_ref, qrep_ref, knew_ref, vnew_ref, ck_ref, cv_ref, o_ref,
                        kbuf_ref, vbuf_ref, sem_ref, s_ref, *, n_past_blocks, while_pages_fly=None):
    b = pl.program_id(0)
    n_pages, _, page = kbuf_ref.shape
    pages_per_block = MOBA_BLOCK // page
    n_rows = qrep_ref.shape[0]

    def page_copy(cache_ref, buf_ref, sem_idx, seq_idx, p):
        return pltpu.make_async_copy(cache_ref.at[pt_ref[seq_idx, p]], buf_ref.at[p], sem_ref.at[sem_idx])

    def for_all_pages(fn):
        lax.fori_loop(0, n_pages, lambda p, c: (fn(p), c)[1], 0)

    @pl.when(b == 0)
    def _first_keys():
        for_all_pages(lambda p: page_copy(ck_ref, kbuf_ref, 0, 0, p).start())

    for_all_pages(lambda p: page_copy(cv_ref, vbuf_ref, 1, b, p).start())

    if while_pages_fly is not None:
        while_pages_fly()

    row = lax.broadcasted_iota(jnp.int32, (n_rows, B_DIM), 0)
    col = lax.broadcasted_iota(jnp.int32, (n_rows, B_DIM), 1)
    own_head = col // B_HEAD_DIM == row % B_HEADS
    q_heads = jnp.where(own_head, qrep_ref[...].astype(F32), 0.0)

    for_all_pages(lambda p: page_copy(ck_ref, kbuf_ref, 0, b, p).wait())

    def score_group(g, c):
        for r in range(SAMPLE_PAGE_GROUP):
            p = g * SAMPLE_PAGE_GROUP + r
            s_ref[p] = _dot(q_heads, kbuf_ref[p])
        return c

    lax.fori_loop(0, n_pages // SAMPLE_PAGE_GROUP, score_group, 0)

    @pl.when(b + 1 < pl.num_programs(0))
    def _next_keys():
        for_all_pages(lambda p: page_copy(ck_ref, kbuf_ref, 0, b + 1, p).start())

    def block_scores(n):
        return [s_ref[pg] for pg in range(n * pages_per_block, (n + 1) * pages_per_block)]

    lane = lax.broadcasted_iota(jnp.int32, (n_rows, LANES), 1)
    qrow = lax.broadcasted_iota(jnp.int32, (n_rows, LANES), 0) // B_HEADS
    gate = jnp.zeros((n_rows, LANES), F32)
    for n in range(n_past_blocks):
        gate = jnp.where(lane == n, jnp.sum(sum(block_scores(n)), axis=-1, keepdims=True), gate)
    sel = _top_k_mask(gate, lane < n_past_blocks, 0, n_past_blocks).astype(F32)

    k_own = jnp.concatenate([knew_ref[...], jnp.zeros((LANES - knew_ref.shape[0], B_DIM), F32)], axis=0)
    own_ok = lane <= qrow
    s_own = jnp.where(own_ok, _dot_nt(q_heads, k_own), NEG_INF)

    sel_cols = [sel[:, n:n + 1] > 0.0 for n in range(n_past_blocks)]
    m_run = jnp.full((n_rows, page), NEG_INF, F32)
    for n in range(n_past_blocks):
        for sn in block_scores(n):
            m_run = jnp.maximum(m_run, jnp.where(sel_cols[n], sn, NEG_INF))
    m = jnp.maximum(jnp.max(m_run, axis=-1, keepdims=True), jnp.max(s_own, axis=-1, keepdims=True))

    p_own = jnp.exp2(s_own - m)
    l_run = jnp.zeros((n_rows, page), F32)
    for pg in range(n_pages):
        pn = jnp.where(sel_cols[pg // pages_per_block], jnp.exp2(s_ref[pg] - m), 0.0)
        s_ref[pg] = pn
        l_run = l_run + pn
    l = jnp.sum(l_run, axis=-1, keepdims=True) + jnp.sum(p_own, axis=-1, keepdims=True)
    acc = jnp.zeros((n_rows, B_DIM), F32)
    for t in range(vnew_ref.shape[0]):
        acc = acc + p_own[:, t:t + 1] * vnew_ref[t:t + 1, :]

    for_all_pages(lambda p: page_copy(cv_ref, vbuf_ref, 1, b, p).wait())

    def value_group(g, acc):
        pv = None
        for r in range(SAMPLE_PAGE_GROUP):
            p = g * SAMPLE_PAGE_GROUP + r
            d = _dot_nt(s_ref[p], vbuf_ref[p])
            pv = d if pv is None else pv + d
        return acc + pv

    acc = lax.fori_loop(0, n_pages // SAMPLE_PAGE_GROUP, value_group, acc)

    out = jnp.where(own_head, acc / l, 0.0)
    n_q = n_rows // B_HEADS
    rows = [jnp.sum(out[t * B_HEADS:(t + 1) * B_HEADS, :], axis=0, keepdims=True) for t in range(n_q)]
    o_ref[...] = jnp.concatenate(rows + [jnp.zeros((o_ref.shape[0] - n_q, B_DIM), F32)], axis=0)


def _moba_kernel(pt_ref, q_ref, kt_ref, vt_ref, qrep_ref, knew_ref, vnew_ref, ck_ref, cv_ref,
                 o_prompt_ref, o_sample_ref, kaug_ref, vaug_ref, kbuf_ref, vbuf_ref, sem_ref, s_ref, *, n_past_blocks):
    _moba_sample_kernel(
        pt_ref, qrep_ref, knew_ref, vnew_ref, ck_ref, cv_ref, o_sample_ref, kbuf_ref, vbuf_ref, sem_ref, s_ref,
        n_past_blocks=n_past_blocks,
        while_pages_fly=functools.partial(_moba_prompt_kernel, q_ref, kt_ref, vt_ref, o_prompt_ref, kaug_ref, vaug_ref))


def _moba(q, kt, vt, page_table, qrep, knew, vnew, cache_k, cache_v, *, batch, seq):
    pairs = B_DIM // LANES
    dec_batch, n_pages = page_table.shape
    page = cache_k.shape[2]
    n_past_blocks = n_pages * page // MOBA_BLOCK
    n_rows, new_pad = qrep.shape[1], knew.shape[1]
    assert n_pages % SAMPLE_PAGE_GROUP == 0 and dec_batch == batch * pairs

    spec = pl.BlockSpec((seq, LANES), lambda i, pt: (i // pairs, i % pairs))
    spec_t = pl.BlockSpec((None, LANES, seq), lambda i, pt: (i // pairs, i % pairs, 0))
    per_seq = lambda rows: pl.BlockSpec((None, rows, B_DIM), lambda i, pt: (i, 0, 0))
    in_hbm = pl.BlockSpec(memory_space=pl.ANY)
    page_buf = pltpu.VMEM((n_pages, B_DIM, page), F32)
    grid_spec = pltpu.PrefetchScalarGridSpec(
        num_scalar_prefetch=1,
        grid=(dec_batch,),
        in_specs=[spec, spec_t, spec_t, per_seq(n_rows), per_seq(new_pad), per_seq(new_pad), in_hbm, in_hbm],
        out_specs=[spec, per_seq(new_pad)],
        scratch_shapes=[pltpu.VMEM((2, LANES, seq), BF16), pltpu.VMEM((2, LANES, seq), BF16),
                        page_buf, page_buf, pltpu.SemaphoreType.DMA((2,)), pltpu.VMEM((n_pages, n_rows, page), F32)],
    )
    return pl.pallas_call(
        functools.partial(_moba_kernel, n_past_blocks=n_past_blocks),
        grid_spec=grid_spec,
        out_shape=[jax.ShapeDtypeStruct((batch * seq, B_DIM), BF16),
                   jax.ShapeDtypeStruct((dec_batch, new_pad, B_DIM), F32)],
        compiler_params=pltpu.CompilerParams(dimension_semantics=("arbitrary",), vmem_limit_bytes=VMEM_LIMIT),
        name="moba",
    )(page_table, q, kt, vt, qrep, knew, vnew, cache_k, cache_v)


def _mlp_ple_kernel(x_ref, ma_ref, mb_ref, p_ref, wout_ref, ln2_ref, wup_ref, wdn_ref, ln3_ref, wg_ref,
                    wpp_ref, pn_ref, y_ref, mix_ref, act_ref):
    mix_ref[:, 0:A_DIM] = ma_ref[...]
    mix_ref[:, A_DIM:A_DIM + B_DIM] = mb_ref[...]
    h = x_ref[...] + _dot(mix_ref[...], wout_ref[...])
    hn = _rms(h, ln2_ref[...]).astype(BF16)
    for c0 in range(0, D_FF, FF_CHUNK):
        act_ref[:, c0:c0 + FF_CHUNK] = jnp.square(jnp.maximum(_dot(hn, wup_ref[:, c0:c0 + FF_CHUNK]), 0.0)).astype(BF16)
    h = h + _dot(act_ref[...], wdn_ref[...])
    gate = 1.0 / (1.0 + jnp.exp(-_dot(_rms(h, ln3_ref[...]).astype(BF16), wg_ref[...])))
    e = _rms(_dot(p_ref[...].astype(BF16), wpp_ref[...]), pn_ref[...])
    y_ref[...] = h + gate * e


def _mlp_ple(x, mix_a, mix_b, p, w_out16, ln2, w_up16, w_down16, ln3, w_gate16, w_proj16, ple_norm, *,
             rows_per_step, name):
    n = x.shape[0]
    row = lambda width: pl.BlockSpec((rows_per_step, width), lambda i: (i, 0))
    return pl.pallas_call(
        _mlp_ple_kernel,
        grid=(n // rows_per_step,),
        in_specs=[row(D_MODEL), row(A_DIM), row(B_DIM), row(D_PLE),
                  _const_spec((A_DIM + B_DIM, D_MODEL)), _const_spec((1, D_MODEL)),
                  _const_spec((D_MODEL, D_FF)), _const_spec((D_FF, D_MODEL)), _const_spec((1, D_MODEL)),
                  _const_spec((D_MODEL, D_MODEL)), _const_spec((D_PLE, D_MODEL)), _const_spec((1, D_MODEL))],
        out_specs=row(D_MODEL),
        out_shape=jax.ShapeDtypeStruct((n, D_MODEL), F32),
        scratch_shapes=[pltpu.VMEM((rows_per_step, A_DIM + B_DIM), BF16), pltpu.VMEM((rows_per_step, D_FF), BF16)],
        compiler_params=pltpu.CompilerParams(dimension_semantics=("arbitrary",), vmem_limit_bytes=VMEM_LIMIT),
        name=name,
    )(x, mix_a, mix_b, p, w_out16, ln2, w_up16, w_down16, ln3, w_gate16, w_proj16, ple_norm)


def kernel(x_prompt, x_sample, p_prompt, p_sample, cache_k, cache_v, page_table, ln1, w_in, a_v_norm, a_ws, a_bs,
           q_norm, k_norm, w_out, ln2, w_up, w_down, ln3, w_ple_gate, w_ple_proj, ple_norm):
    depth = ln1.shape[0]
    batch, seq, _ = x_prompt.shape
    dec_batch, n_new, _ = x_sample.shape
    n_pool, page = cache_k.shape[1], cache_k.shape[2]
    n_p, n_s = batch * seq, dec_batch * n_new
    assert n_s == A_CHUNK and n_new <= A_CHUNK and A_CHUNK % n_new == 0 and seq % ROW_TILE == 0
    assert MOBA_BLOCK % page == 0 and (page_table.shape[1] * page) % MOBA_BLOCK == 0

    hp = x_prompt.reshape(n_p, D_MODEL)
    hs = x_sample.reshape(n_s, D_MODEL)
    kp_l, vp_l, ks_l, vs_l, as_l = [], [], [], [], []
    for i in range(depth):
        row = lambda a: a.reshape(1, -1)
        w_in16, w_out16, w_up16, w_down16 = (w[i].astype(BF16) for w in (w_in, w_out, w_up, w_down))
        w_gate16, w_proj16 = w_ple_gate[i].astype(BF16), w_ple_proj[i].astype(BF16)
        avn = row(a_v_norm[i])
        qn, kn = row(jnp.tile(q_norm[i], B_HEADS)), row(jnp.tile(k_norm[i], B_HEADS))
        mix_bias = jnp.broadcast_to(a_bs[i][:, :, None], (A_GROUPS, A_CHUNK, A_WIDTH))

        mix_a, q, kt, vt = _in_proj(hp, row(ln1[i]), w_in16, avn, qn, kn, a_ws[i], mix_bias,
                                    rows_per_step=ROW_TILE, mix_block=A_CHUNK, seq=seq)
        mix_as, qs, k, v, va = _in_proj(hs, row(ln1[i]), w_in16, avn, qn, kn, a_ws[i], mix_bias,
                                        rows_per_step=n_s, mix_block=n_new)

        new_pad = 8
        pad_new = lambda a: jnp.pad(a.reshape(dec_batch, n_new, B_DIM), ((0, 0), (0, new_pad - n_new), (0, 0)))
        qrep = jnp.repeat(qs.reshape(dec_batch, n_new, B_DIM), B_HEADS, axis=1)
        pages_t = lambda c: c.transpose(0, 2, 3, 1).reshape(n_pool, B_DIM, page)
        mix_b, o = _moba(q, kt, vt, page_table, qrep, pad_new(k), pad_new(v), pages_t(cache_k[i]),
                         pages_t(cache_v[i]), batch=batch, seq=seq)
        mix_bs = o[:, :n_new, :].reshape(n_s, B_DIM).astype(BF16)

        hp = _mlp_ple(hp, mix_a, mix_b, p_prompt[i].reshape(n_p, D_PLE), w_out16, row(ln2[i]), w_up16, w_down16,
                      row(ln3[i]), w_gate16, w_proj16, row(ple_norm[i]), rows_per_step=ROW_TILE, name="mlp_ple_prompt")
        hs = _mlp_ple(hs, mix_as, mix_bs, p_sample[i].reshape(n_s, D_PLE), w_out16, row(ln2[i]), w_up16, w_down16,
                      row(ln3[i]), w_gate16, w_proj16, row(ple_norm[i]), rows_per_step=n_s, name="mlp_ple_sample")
        heads_last = lambda t: t.reshape(batch, B_HEADS, B_HEAD_DIM, seq).transpose(0, 3, 1, 2)
        kp_l.append(heads_last(kt))
        vp_l.append(heads_last(vt))
        ks_l.append(k.reshape(dec_batch, n_new, B_HEADS, B_HEAD_DIM))
        vs_l.append(v.reshape(dec_batch, n_new, B_HEADS, B_HEAD_DIM))
        as_l.append(va.reshape(dec_batch, n_new, A_GROUPS, A_WIDTH))

    return (hp.reshape(batch, seq, D_MODEL), hs.reshape(dec_batch, n_new, D_MODEL),
            jnp.stack(kp_l), jnp.stack(vp_l), jnp.stack(ks_l), jnp.stack(vs_l), jnp.stack(as_l))
```

```python
import functools

import jax
import jax.numpy as jnp
from jax import lax
from jax.experimental import pallas as pl
from jax.experimental.pallas import tpu as pltpu

D_MODEL = 1024
D_PLE = 256
A_GROUPS = 4
A_WIDTH = 128
A_CHUNK = 128
A_DIM = A_GROUPS * A_WIDTH
B_HEADS = 8
B_HEAD_DIM = 64
B_DIM = B_HEADS * B_HEAD_DIM
MOBA_BLOCK = 256
MOBA_TOPK = 3
D_FF = 4 * D_MODEL
IN_DIM = 2 * A_DIM + 3 * B_DIM
EPS = 1e-6

LANES = 128
ROW_TILE = 512
IN_PROJ_ROW_TILE = 1024
FF_CHUNK = 1024
SAMPLE_PAGE_GROUP = 8
VMEM_LIMIT = 56 * 1024 * 1024

BF16 = jnp.bfloat16
F32 = jnp.float32
NEG_INF = float("-inf")
LOG2_E = 1.4426950408889634
MASK_BIAS = -1e30


def _dot(a, b):
    return jnp.dot(a, b, preferred_element_type=F32)


def _dot_nt(a, b):
    return lax.dot_general(a, b, (((1,), (1,)), ((), ())), preferred_element_type=F32)


def _rms(x, gain):
    return x * lax.rsqrt(jnp.mean(x * x, axis=-1, keepdims=True) + EPS) * gain


def _const_spec(shape):
    nd = len(shape)
    return pl.BlockSpec(shape, lambda *_: (0,) * nd, pipeline_mode=pl.Buffered(1))


def _head_rms(z, gain_row):
    lane = lax.broadcasted_iota(jnp.int32, (1, LANES), 1)
    lo = lane < B_HEAD_DIM
    outs = []
    for t in range(B_DIM // LANES):
        zt = z[:, t * LANES:(t + 1) * LANES]
        z2 = zt * zt
        s_lo = jnp.sum(jnp.where(lo, z2, 0.0), axis=-1, keepdims=True)
        s_hi = jnp.sum(jnp.where(lo, 0.0, z2), axis=-1, keepdims=True)
        ms = jnp.where(lo, s_lo, s_hi) * (1.0 / B_HEAD_DIM)
        outs.append(zt * lax.rsqrt(ms + EPS))
    return jnp.concatenate(outs, axis=-1) * gain_row


def _tile_corner(m, blk, tile_lanes):
    r = lax.broadcasted_iota(jnp.int32, m.shape, 0) % blk
    c = lax.broadcasted_iota(jnp.int32, m.shape, 1) % blk
    out = jnp.zeros_like(m)
    for t in range(blk):
        out = jnp.where(r == t, m[t:t + 1, :], out)
    if tile_lanes:
        by_rows, out = out, jnp.zeros_like(m)
        for t in range(blk):
            out = jnp.where(c == t, by_rows[:, t:t + 1], out)
    return out


def _in_proj_kernel(x_ref, ln1_ref, w_ref, avn_ref, qn_ref, kn_ref, mixw_ref, mixb_ref,
                    *out_refs, mix_block, prompt):
    if prompt:
        mixa_ref, q_ref, k_ref, v_ref = out_refs
    else:
        mixa_ref, q_ref, k_ref, v_ref, va_ref = out_refs
    rows = x_ref.shape[0]
    hn = _rms(x_ref[...], ln1_ref[...]).astype(BF16)

    def proj(c0, n):
        return _dot(hn, w_ref[:, c0:c0 + n])

    u = jax.nn.gelu(proj(0, A_DIM))
    gv = jax.nn.gelu(proj(A_DIM, A_DIM))
    va = jnp.concatenate(
        [_rms(gv[:, g * A_WIDTH:(g + 1) * A_WIDTH], 1.0) for g in range(A_GROUPS)], axis=-1) * avn_ref[...]
    if not prompt:
        va_ref[...] = va

    r = lax.broadcasted_iota(jnp.int32, (A_CHUNK, A_CHUNK), 0)
    c = lax.broadcasted_iota(jnp.int32, (A_CHUNK, A_CHUNK), 1)
    keep = (r // mix_block == c // mix_block) & (c % mix_block <= r % mix_block)
    va16 = va.astype(BF16)
    for g in range(A_GROUPS):
        wg, bg = mixw_ref[g], mixb_ref[g]
        if mix_block != A_CHUNK:
            wg, bg = _tile_corner(wg, mix_block, True), _tile_corner(bg, mix_block, False)
        wg = jnp.where(keep, wg, 0.0).astype(BF16)
        for ch in range(rows // A_CHUNK):
            rs = slice(ch * A_CHUNK, (ch + 1) * A_CHUNK)
            cs = slice(g * A_WIDTH, (g + 1) * A_WIDTH)
            sg = _dot(wg, va16[rs, cs]) + bg
            mixa_ref[rs, cs] = (u[rs, cs] * sg).astype(BF16)

    q0 = 2 * A_DIM
    q = _head_rms(proj(q0, B_DIM), qn_ref[...])
    q_ref[...] = (q * (B_HEAD_DIM ** -0.5 * LOG2_E)).astype(BF16)
    k = _head_rms(proj(q0 + B_DIM, B_DIM), kn_ref[...])
    v = proj(q0 + 2 * B_DIM, B_DIM)
    if prompt:
        k_ref[...] = k.T
        v_ref[...] = v.T
    else:
        k_ref[...] = k
        v_ref[...] = v


def _in_proj(x, ln1, w_in16, avn, qn, kn, mixw, mixb, *, rows_per_step, mix_block, seq=None):
    n = x.shape[0]
    prompt = seq is not None
    row = lambda width: pl.BlockSpec((rows_per_step, width), lambda i: (i, 0))
    out_shape = [jax.ShapeDtypeStruct((n, A_DIM), BF16), jax.ShapeDtypeStruct((n, B_DIM), BF16)]
    out_specs = [row(A_DIM), row(B_DIM)]
    if prompt:
        steps = seq // rows_per_step
        kv_t = pl.BlockSpec((None, B_DIM, rows_per_step), lambda i: (i // steps, 0, i % steps))
        out_shape += [jax.ShapeDtypeStruct((n // seq, B_DIM, seq), F32)] * 2
        out_specs += [kv_t, kv_t]
    else:
        out_shape += [jax.ShapeDtypeStruct((n, B_DIM), F32)] * 2 + [jax.ShapeDtypeStruct((n, A_DIM), F32)]
        out_specs += [row(B_DIM), row(B_DIM), row(A_DIM)]
    return pl.pallas_call(
        functools.partial(_in_proj_kernel, mix_block=mix_block, prompt=prompt),
        grid=(n // rows_per_step,),
        in_specs=[row(D_MODEL), _const_spec((1, D_MODEL)), _const_spec((D_MODEL, IN_DIM)),
                  _const_spec((1, A_DIM)), _const_spec((1, B_DIM)), _const_spec((1, B_DIM)),
                  _const_spec((A_GROUPS, A_CHUNK, A_CHUNK)), _const_spec((A_GROUPS, A_CHUNK, A_WIDTH))],
        out_specs=out_specs,
        out_shape=out_shape,
        compiler_params=pltpu.CompilerParams(dimension_semantics=("arbitrary",), vmem_limit_bytes=VMEM_LIMIT),
        name="in_proj_prompt" if prompt else "in_proj_sample",
    )(x, ln1, w_in16, avn, qn, kn, mixw, mixb)


def _top_k_mask(g, valid, lane0, n_cand):
    lane = lax.broadcasted_iota(jnp.int32, g.shape, 1)
    gm = jnp.where(valid, g, NEG_INF)
    rank = jnp.zeros(g.shape, jnp.int32)
    for m in range(lane0, lane0 + n_cand):
        col = gm[:, m:m + 1]
        beats = (col > gm) | ((col == gm) & (lane > m))
        rank = rank + beats.astype(jnp.int32)
    return valid & (rank < MOBA_TOPK)


def _top_k_mask_pair(g, valid, n_cand):
    lane = lax.broadcasted_iota(jnp.int32, g.shape, 1)
    upper = lane >= B_HEAD_DIM
    idx = lane % B_HEAD_DIM
    gm = jnp.where(valid, g, NEG_INF)
    rank = jnp.zeros(g.shape, jnp.int32)
    for m in range(n_cand):
        col = jnp.where(upper, gm[:, B_HEAD_DIM + m:B_HEAD_DIM + m + 1], gm[:, m:m + 1])
        beats = (col > gm) | ((col == gm) & (idx > m))
        rank = rank + beats.astype(jnp.int32)
    return valid & (rank < MOBA_TOPK)


def _moba_prompt_kernel(q_ref, kt_ref, vt_ref, o_ref, kaug_ref, vaug_ref):
    seq = q_ref.shape[0]
    n_blocks = seq // MOBA_BLOCK
    heads = ((0, B_HEAD_DIM), (B_HEAD_DIM, 0))

    kt = kt_ref[...]
    vt = vt_ref[...]
    chan_s = lax.broadcasted_iota(jnp.int32, (LANES, seq), 0)
    blk_s = lax.broadcasted_iota(jnp.int32, (LANES, seq), 1) // MOBA_BLOCK
    lane_c = lax.broadcasted_iota(jnp.int32, (LANES, LANES), 1)
    kmean = [jnp.mean(kt_ref[:, j * MOBA_BLOCK:(j + 1) * MOBA_BLOCK], axis=1, keepdims=True)
             for j in range(n_blocks)]
    chan_c = lax.broadcasted_iota(jnp.int32, (LANES, LANES), 0)
    gate_w = jnp.zeros((LANES, LANES), F32)
    for h, (d0, a0) in enumerate(heads):
        in_head = (chan_s >= d0) & (chan_s < d0 + B_HEAD_DIM)
        kaug_ref[h] = jnp.where(in_head, kt, jnp.where(chan_s - a0 == blk_s, 1.0, 0.0)).astype(BF16)
        vaug_ref[h] = jnp.where(in_head, vt, jnp.where(chan_s == a0, 1.0, 0.0)).astype(BF16)
        in_head_c = (chan_c >= d0) & (chan_c < d0 + B_HEAD_DIM)
        for n in range(n_blocks):
            gate_w = jnp.where((lane_c == a0 + n) & in_head_c, kmean[n], gate_w)
    gate_w = gate_w.astype(BF16)

    lane_q = lax.broadcasted_iota(jnp.int32, (MOBA_BLOCK, LANES), 1)
    kpos = lax.broadcasted_iota(jnp.int32, (MOBA_BLOCK, MOBA_BLOCK), 1)
    qpos = lax.broadcasted_iota(jnp.int32, (MOBA_BLOCK, MOBA_BLOCK), 0)
    causal = kpos <= qpos

    for i in range(n_blocks):
        rows = slice(i * MOBA_BLOCK, (i + 1) * MOBA_BLOCK)
        n_keys = (i + 1) * MOBA_BLOCK
        q_pair = q_ref[rows, :]
        bias = jnp.zeros_like(q_pair)
        if i > MOBA_TOPK:
            past = lane_q % B_HEAD_DIM < i
            sel = _top_k_mask_pair(_dot(q_pair, gate_w), past, i)
            bias = jnp.where(past & ~sel, MASK_BIAS, 0.0).astype(BF16)
        outs = []
        for h, (d0, a0) in enumerate(heads):
            in_head = (lane_q >= d0) & (lane_q < d0 + B_HEAD_DIM)
            q_aug = jnp.where(in_head, q_pair, bias)
            s = _dot(q_aug, kaug_ref[h, :, 0:n_keys])
            s_own = jnp.where(causal, s[:, n_keys - MOBA_BLOCK:], MASK_BIAS)
            s = jnp.concatenate([s[:, :n_keys - MOBA_BLOCK], s_own], axis=-1) if i else s_own
            p = jnp.exp2(s - jnp.max(s, axis=-1, keepdims=True)).astype(BF16)
            acc = _dot_nt(p, vaug_ref[h, :, 0:n_keys])
            outs.append(acc / acc[:, a0:a0 + 1])
        o_ref[rows, :] = jnp.where(lane_q < B_HEAD_DIM, outs[0], outs[1]).astype(o_ref.dtype)


def _moba_sample_kernel(pt_ref, qrep_ref, knew_ref, vnew_ref, ck_ref, cv_ref, o_ref,
                        kbuf_ref, vbuf_ref, sem_ref, s_ref, *, n_past_blocks, while_pages_fly=None):
    b = pl.program_id(0)
    n_pages, _, page = kbuf_ref.shape
    pages_per_block = MOBA_BLOCK // page
    n_rows = qrep_ref.shape[0]

    def page_copy(cache_ref, buf_ref, sem_idx, seq_idx, p):
        return pltpu.make_async_copy(cache_ref.at[pt_ref[seq_idx, p]], buf_ref.at[p], sem_ref.at[sem_idx])

    def for_all_pages(fn):
        lax.fori_loop(0, n_pages, lambda p, c: (fn(p), c)[1], 0)

    @pl.when(b == 0)
    def _first_keys():
        for_all_pages(lambda p: page_copy(ck_ref, kbuf_ref, 0, 0, p).start())

    for_all_pages(lambda p: page_copy(cv_ref, vbuf_ref, 1, b, p).start())

    if while_pages_fly is not None:
        while_pages_fly()

    row = lax.broadcasted_iota(jnp.int32, (n_rows, B_DIM), 0)
    col = lax.broadcasted_iota(jnp.int32, (n_rows, B_DIM), 1)
    own_head = col // B_HEAD_DIM == row % B_HEADS
    q_heads = jnp.where(own_head, qrep_ref[...].astype(F32), 0.0)

    for_all_pages(lambda p: page_copy(ck_ref, kbuf_ref, 0, b, p).wait())

    def score_group(g, c):
        for r in range(SAMPLE_PAGE_GROUP):
            p = g * SAMPLE_PAGE_GROUP + r
            s_ref[p] = _dot(q_heads, kbuf_ref[p])
        return c

    lax.fori_loop(0, n_pages // SAMPLE_PAGE_GROUP, score_group, 0)

    @pl.when(b + 1 < pl.num_programs(0))
    def _next_keys():
        for_all_pages(lambda p: page_copy(ck_ref, kbuf_ref, 0, b + 1, p).start())

    def block_scores(n):
        return [s_ref[pg] for pg in range(n * pages_per_block, (n + 1) * pages_per_block)]

    lane = lax.broadcasted_iota(jnp.int32, (n_rows, LANES), 1)
    qrow = lax.broadcasted_iota(jnp.int32, (n_rows, LANES), 0) // B_HEADS
    gate = jnp.zeros((n_rows, LANES), F32)
    for n in range(n_past_blocks):
        gate = jnp.where(lane == n, jnp.sum(sum(block_scores(n)), axis=-1, keepdims=True), gate)
    sel = _top_k_mask(gate, lane < n_past_blocks, 0, n_past_blocks).astype(F32)

    k_own = jnp.concatenate([knew_ref[...], jnp.zeros((LANES - knew_ref.shape[0], B_DIM), F32)], axis=0)
    own_ok = lane <= qrow
    s_own = jnp.where(own_ok, _dot_nt(q_heads, k_own), NEG_INF)

    sel_cols = [sel[:, n:n + 1] > 0.0 for n in range(n_past_blocks)]
    m_run = jnp.full((n_rows, page), NEG_INF, F32)
    for n in range(n_past_blocks):
        for sn in block_scores(n):
            m_run = jnp.maximum(m_run, jnp.where(sel_cols[n], sn, NEG_INF))
    m = jnp.maximum(jnp.max(m_run, axis=-1, keepdims=True), jnp.max(s_own, axis=-1, keepdims=True))

    p_own = jnp.exp2(s_own - m)
    l_run = jnp.zeros((n_rows, page), F32)
    for pg in range(n_pages):
        pn = jnp.where(sel_cols[pg // pages_per_block], jnp.exp2(s_ref[pg] - m), 0.0)
        s_ref[pg] = pn
        l_run = l_run + pn
    l = jnp.sum(l_run, axis=-1, keepdims=True) + jnp.sum(p_own, axis=-1, keepdims=True)
    acc = jnp.zeros((n_rows, B_DIM), F32)
    for t in range(vnew_ref.shape[0]):
        acc = acc + p_own[:, t:t + 1] * vnew_ref[t:t + 1, :]

    for_all_pages(lambda p: page_copy(cv_ref, vbuf_ref, 1, b, p).wait())

    def value_group(g, acc):
        pv = None
        for r in range(SAMPLE_PAGE_GROUP):
            p = g * SAMPLE_PAGE_GROUP + r
            d = _dot_nt(s_ref[p], vbuf_ref[p])
            pv = d if pv is None else pv + d
        return acc + pv

    acc = lax.fori_loop(0, n_pages // SAMPLE_PAGE_GROUP, value_group, acc)

    out = jnp.where(own_head, acc / l, 0.0)
    n_q = n_rows // B_HEADS
    rows = [jnp.sum(out[t * B_HEADS:(t + 1) * B_HEADS, :], axis=0, keepdims=True) for t in range(n_q)]
    o_ref[...] = jnp.concatenate(rows + [jnp.zeros((o_ref.shape[0] - n_q, B_DIM), F32)], axis=0)


def _moba_kernel(pt_ref, q_ref, kt_ref, vt_ref, qrep_ref, knew_ref, vnew_ref, ck_ref, cv_ref,
                 o_prompt_ref, o_sample_ref, kaug_ref, vaug_ref, kbuf_ref, vbuf_ref, sem_ref, s_ref, *, n_past_blocks):
    _moba_sample_kernel(
        pt_ref, qrep_ref, knew_ref, vnew_ref, ck_ref, cv_ref, o_sample_ref, kbuf_ref, vbuf_ref, sem_ref, s_ref,
        n_past_blocks=n_past_blocks,
        while_pages_fly=functools.partial(_moba_prompt_kernel, q_ref, kt_ref, vt_ref, o_prompt_ref, kaug_ref, vaug_ref))


def _moba(q, kt, vt, page_table, qrep, knew, vnew, cache_k, cache_v, *, batch, seq):
    pairs = B_DIM // LANES
    dec_batch, n_pages = page_table.shape
    page = cache_k.shape[2]
    n_past_blocks = n_pages * page // MOBA_BLOCK
    n_rows, new_pad = qrep.shape[1], knew.shape[1]
    assert n_pages % SAMPLE_PAGE_GROUP == 0 and dec_batch == batch * pairs

    spec = pl.BlockSpec((seq, LANES), lambda i, pt: (i // pairs, i % pairs))
    spec_t = pl.BlockSpec((None, LANES, seq), lambda i, pt: (i // pairs, i % pairs, 0))
    per_seq = lambda rows: pl.BlockSpec((None, rows, B_DIM), lambda i, pt: (i, 0, 0))
    in_hbm = pl.BlockSpec(memory_space=pl.ANY)
    page_buf = pltpu.VMEM((n_pages, B_DIM, page), F32)
    grid_spec = pltpu.PrefetchScalarGridSpec(
        num_scalar_prefetch=1,
        grid=(dec_batch,),
        in_specs=[spec, spec_t, spec_t, per_seq(n_rows), per_seq(new_pad), per_seq(new_pad), in_hbm, in_hbm],
        out_specs=[spec, per_seq(new_pad)],
        scratch_shapes=[pltpu.VMEM((2, LANES, seq), BF16), pltpu.VMEM((2, LANES, seq), BF16),
                        page_buf, page_buf, pltpu.SemaphoreType.DMA((2,)), pltpu.VMEM((n_pages, n_rows, page), F32)],
    )
    return pl.pallas_call(
        functools.partial(_moba_kernel, n_past_blocks=n_past_blocks),
        grid_spec=grid_spec,
        out_shape=[jax.ShapeDtypeStruct((batch * seq, B_DIM), BF16),
                   jax.ShapeDtypeStruct((dec_batch, new_pad, B_DIM), F32)],
        compiler_params=pltpu.CompilerParams(dimension_semantics=("arbitrary",), vmem_limit_bytes=VMEM_LIMIT),
        name="moba",
    )(page_table, q, kt, vt, qrep, knew, vnew, cache_k, cache_v)


def _mlp_ple_kernel(x_ref, ma_ref, mb_ref, p_ref, wout_ref, ln2_ref, wup_ref, wdn_ref, ln3_ref, wg_ref,
                    wpp_ref, pn_ref, y_ref, mix_ref, act_ref):
    mix_ref[:, 0:A_DIM] = ma_ref[...]
    mix_ref[:, A_DIM:A_DIM + B_DIM] = mb_ref[...]
    h = x_ref[...] + _dot(mix_ref[...], wout_ref[...])
    hn = _rms(h, ln2_ref[...]).astype(BF16)
    for c0 in range(0, D_FF, FF_CHUNK):
        act_ref[:, c0:c0 + FF_CHUNK] = jnp.square(jnp.maximum(_dot(hn, wup_ref[:, c0:c0 + FF_CHUNK]), 0.0)).astype(BF16)
    h = h + _dot(act_ref[...], wdn_ref[...])
    gate = 1.0 / (1.0 + jnp.exp(-_dot(_rms(h, ln3_ref[...]).astype(BF16), wg_ref[...])))
    e = _rms(_dot(p_ref[...].astype(BF16), wpp_ref[...]), pn_ref[...])
    y_ref[...] = h + gate * e


def _mlp_ple(x, mix_a, mix_b, p, w_out16, ln2, w_up16, w_down16, ln3, w_gate16, w_proj16, ple_norm, *,
             rows_per_step, name):
    n = x.shape[0]
    row = lambda width: pl.BlockSpec((rows_per_step, width), lambda i: (i, 0))
    return pl.pallas_call(
        _mlp_ple_kernel,
        grid=(n // rows_per_step,),
        in_specs=[row(D_MODEL), row(A_DIM), row(B_DIM), row(D_PLE),
                  _const_spec((A_DIM + B_DIM, D_MODEL)), _const_spec((1, D_MODEL)),
                  _const_spec((D_MODEL, D_FF)), _const_spec((D_FF, D_MODEL)), _const_spec((1, D_MODEL)),
                  _const_spec((D_MODEL, D_MODEL)), _const_spec((D_PLE, D_MODEL)), _const_spec((1, D_MODEL))],
        out_specs=row(D_MODEL),
        out_shape=jax.ShapeDtypeStruct((n, D_MODEL), F32),
        scratch_shapes=[pltpu.VMEM((rows_per_step, A_DIM + B_DIM), BF16), pltpu.VMEM((rows_per_step, D_FF), BF16)],
        compiler_params=pltpu.CompilerParams(dimension_semantics=("arbitrary",), vmem_limit_bytes=VMEM_LIMIT),
        name=name,
    )(x, mix_a, mix_b, p, w_out16, ln2, w_up16, w_down16, ln3, w_gate16, w_proj16, ple_norm)


def kernel(x_prompt, x_sample, p_prompt, p_sample, cache_k, cache_v, page_table, ln1, w_in, a_v_norm, a_ws, a_bs,
           q_norm, k_norm, w_out, ln2, w_up, w_down, ln3, w_ple_gate, w_ple_proj, ple_norm):
    depth = ln1.shape[0]
    batch, seq, _ = x_prompt.shape
    dec_batch, n_new, _ = x_sample.shape
    n_pool, page = cache_k.shape[1], cache_k.shape[2]
    n_p, n_s = batch * seq, dec_batch * n_new
    assert n_s == A_CHUNK and n_new <= A_CHUNK and A_CHUNK % n_new == 0 and seq % IN_PROJ_ROW_TILE == 0
    assert MOBA_BLOCK % page == 0 and (page_table.shape[1] * page) % MOBA_BLOCK == 0

    hp = x_prompt.reshape(n_p, D_MODEL)
    hs = x_sample.reshape(n_s, D_MODEL)
    kp_l, vp_l, ks_l, vs_l, as_l = [], [], [], [], []
    for i in range(depth):
        row = lambda a: a.reshape(1, -1)
        w_in16, w_out16, w_up16, w_down16 = (w[i].astype(BF16) for w in (w_in, w_out, w_up, w_down))
        w_gate16, w_proj16 = w_ple_gate[i].astype(BF16), w_ple_proj[i].astype(BF16)
        avn = row(a_v_norm[i])
        qn, kn = row(jnp.tile(q_norm[i], B_HEADS)), row(jnp.tile(k_norm[i], B_HEADS))
        mix_bias = jnp.broadcast_to(a_bs[i][:, :, None], (A_GROUPS, A_CHUNK, A_WIDTH))

        mix_a, q, kt, vt = _in_proj(hp, row(ln1[i]), w_in16, avn, qn, kn, a_ws[i], mix_bias,
                                    rows_per_step=IN_PROJ_ROW_TILE, mix_block=A_CHUNK, seq=seq)
        mix_as, qs, k, v, va = _in_proj(hs, row(ln1[i]), w_in16, avn, qn, kn, a_ws[i], mix_bias,
                                        rows_per_step=n_s, mix_block=n_new)

        new_pad = 8
        pad_new = lambda a: jnp.pad(a.reshape(dec_batch, n_new, B_DIM), ((0, 0), (0, new_pad - n_new), (0, 0)))
        qrep = jnp.repeat(qs.reshape(dec_batch, n_new, B_DIM), B_HEADS, axis=1)
        pages_t = lambda c: c.transpose(0, 2, 3, 1).reshape(n_pool, B_DIM, page)
        mix_b, o = _moba(q, kt, vt, page_table, qrep, pad_new(k), pad_new(v), pages_t(cache_k[i]),
                         pages_t(cache_v[i]), batch=batch, seq=seq)
        mix_bs = o[:, :n_new, :].reshape(n_s, B_DIM).astype(BF16)

        hp = _mlp_ple(hp, mix_a, mix_b, p_prompt[i].reshape(n_p, D_PLE), w_out16, row(ln2[i]), w_up16, w_down16,
                      row(ln3[i]), w_gate16, w_proj16, row(ple_norm[i]), rows_per_step=ROW_TILE, name="mlp_ple_prompt")
        hs = _mlp_ple(hs, mix_as, mix_bs, p_sample[i].reshape(n_s, D_PLE), w_out16, row(ln2[i]), w_up16, w_down16,
                      row(ln3[i]), w_gate16, w_proj16, row(ple_norm[i]), rows_per_step=n_s, name="mlp_ple_sample")
        heads_last = lambda t: t.reshape(batch, B_HEADS, B_HEAD_DIM, seq).transpose(0, 3, 1, 2)
        kp_l.append(heads_last(kt))
        vp_l.append(heads_last(vt))
        ks_l.append(k.reshape(dec_batch, n_new, B_HEADS, B_HEAD_DIM))
        vs_l.append(v.reshape(dec_batch, n_new, B_HEADS, B_HEAD_DIM))
        as_l.append(va.reshape(dec_batch, n_new, A_GROUPS, A_WIDTH))

    return (hp.reshape(batch, seq, D_MODEL), hs.reshape(dec_batch, n_new, D_MODEL),
            jnp.stack(kp_l), jnp.stack(vp_l), jnp.stack(ks_l), jnp.stack(vs_l), jnp.stack(as_l))
```

```python
import functools

import jax
import jax.numpy as jnp
from jax import lax
from jax.experimental import pallas as pl
from jax.experimental.pallas import tpu as pltpu

D_MODEL = 1024
D_PLE = 256
A_GROUPS = 4
A_WIDTH = 128
A_CHUNK = 128
A_DIM = A_GROUPS * A_WIDTH
B_HEADS = 8
B_HEAD_DIM = 64
B_DIM = B_HEADS * B_HEAD_DIM
MOBA_BLOCK = 256
MOBA_TOPK = 3
D_FF = 4 * D_MODEL
IN_DIM = 2 * A_DIM + 3 * B_DIM
EPS = 1e-6

LANES = 128
ROW_TILE = 512
IN_PROJ_ROW_TILE = 1024
FF_CHUNK = 1024
SAMPLE_PAGE_GROUP = 8
VMEM_LIMIT = 56 * 1024 * 1024

BF16 = jnp.bfloat16
F32 = jnp.float32
NEG_INF = float("-inf")
LOG2_E = 1.4426950408889634
MASK_BIAS = -1e30


def _dot(a, b):
    return jnp.dot(a, b, preferred_element_type=F32)


def _dot_nt(a, b):
    return lax.dot_general(a, b, (((1,), (1,)), ((), ())), preferred_element_type=F32)


def _rms(x, gain):
    return x * lax.rsqrt(jnp.mean(x * x, axis=-1, keepdims=True) + EPS) * gain


def _const_spec(shape):
    nd = len(shape)
    return pl.BlockSpec(shape, lambda *_: (0,) * nd, pipeline_mode=pl.Buffered(1))


def _head_rms(z, gain_row):
    lane = lax.broadcasted_iota(jnp.int32, (1, LANES), 1)
    lo = lane < B_HEAD_DIM
    outs = []
    for t in range(B_DIM // LANES):
        zt = z[:, t * LANES:(t + 1) * LANES]
        z2 = zt * zt
        s_lo = jnp.sum(jnp.where(lo, z2, 0.0), axis=-1, keepdims=True)
        s_hi = jnp.sum(jnp.where(lo, 0.0, z2), axis=-1, keepdims=True)
        ms = jnp.where(lo, s_lo, s_hi) * (1.0 / B_HEAD_DIM)
        outs.append(zt * lax.rsqrt(ms + EPS))
    return jnp.concatenate(outs, axis=-1) * gain_row


def _tile_corner(m, blk, tile_lanes):
    r = lax.broadcasted_iota(jnp.int32, m.shape, 0) % blk
    c = lax.broadcasted_iota(jnp.int32, m.shape, 1) % blk
    out = jnp.zeros_like(m)
    for t in range(blk):
        out = jnp.where(r == t, m[t:t + 1, :], out)
    if tile_lanes:
        by_rows, out = out, jnp.zeros_like(m)
        for t in range(blk):
            out = jnp.where(c == t, by_rows[:, t:t + 1], out)
    return out


def _in_proj_kernel(x_ref, ln1_ref, w_ref, avn_ref, qn_ref, kn_ref, mixw_ref, mixb_ref,
                    *out_refs, mix_block, prompt):
    if prompt:
        mixa_ref, q_ref, k_ref, v_ref = out_refs
    else:
        mixa_ref, q_ref, k_ref, v_ref, va_ref = out_refs
    rows = x_ref.shape[0]
    hn = _rms(x_ref[...], ln1_ref[...]).astype(BF16)

    def proj(c0, n):
        return _dot(hn, w_ref[:, c0:c0 + n])

    u = jax.nn.gelu(proj(0, A_DIM))
    gv = jax.nn.gelu(proj(A_DIM, A_DIM))
    va = jnp.concatenate(
        [_rms(gv[:, g * A_WIDTH:(g + 1) * A_WIDTH], 1.0) for g in range(A_GROUPS)], axis=-1) * avn_ref[...]
    if not prompt:
        va_ref[...] = va

    r = lax.broadcasted_iota(jnp.int32, (A_CHUNK, A_CHUNK), 0)
    c = lax.broadcasted_iota(jnp.int32, (A_CHUNK, A_CHUNK), 1)
    keep = (r // mix_block == c // mix_block) & (c % mix_block <= r % mix_block)
    va16 = va.astype(BF16)
    for g in range(A_GROUPS):
        wg, bg = mixw_ref[g], mixb_ref[g]
        if mix_block != A_CHUNK:
            wg, bg = _tile_corner(wg, mix_block, True), _tile_corner(bg, mix_block, False)
        wg = jnp.where(keep, wg, 0.0).astype(BF16)
        for ch in range(rows // A_CHUNK):
            rs = slice(ch * A_CHUNK, (ch + 1) * A_CHUNK)
            cs = slice(g * A_WIDTH, (g + 1) * A_WIDTH)
            sg = _dot(wg, va16[rs, cs]) + bg
            mixa_ref[rs, cs] = (u[rs, cs] * sg).astype(BF16)

    q0 = 2 * A_DIM
    q = _head_rms(proj(q0, B_DIM), qn_ref[...])
    q_ref[...] = (q * (B_HEAD_DIM ** -0.5 * LOG2_E)).astype(BF16)
    k = _head_rms(proj(q0 + B_DIM, B_DIM), kn_ref[...])
    v = proj(q0 + 2 * B_DIM, B_DIM)
    if prompt:
        k_ref[...] = k.T
        v_ref[...] = v.T
    else:
        k_ref[...] = k
        v_ref[...] = v


def _in_proj(x, ln1, w_in16, avn, qn, kn, mixw, mixb, *, rows_per_step, mix_block, seq=None):
    n = x.shape[0]
    prompt = seq is not None
    row = lambda width: pl.BlockSpec((rows_per_step, width), lambda i: (i, 0))
    out_shape = [jax.ShapeDtypeStruct((n, A_DIM), BF16), jax.ShapeDtypeStruct((n, B_DIM), BF16)]
    out_specs = [row(A_DIM), row(B_DIM)]
    if prompt:
        steps = seq // rows_per_step
        kv_t = pl.BlockSpec((None, B_DIM, rows_per_step), lambda i: (i // steps, 0, i % steps))
        out_shape += [jax.ShapeDtypeStruct((n // seq, B_DIM, seq), F32)] * 2
        out_specs += [kv_t, kv_t]
    else:
        out_shape += [jax.ShapeDtypeStruct((n, B_DIM), F32)] * 2 + [jax.ShapeDtypeStruct((n, A_DIM), F32)]
        out_specs += [row(B_DIM), row(B_DIM), row(A_DIM)]
    return pl.pallas_call(
        functools.partial(_in_proj_kernel, mix_block=mix_block, prompt=prompt),
        grid=(n // rows_per_step,),
        in_specs=[row(D_MODEL), _const_spec((1, D_MODEL)), _const_spec((D_MODEL, IN_DIM)),
                  _const_spec((1, A_DIM)), _const_spec((1, B_DIM)), _const_spec((1, B_DIM)),
                  _const_spec((A_GROUPS, A_CHUNK, A_CHUNK)), _const_spec((A_GROUPS, A_CHUNK, A_WIDTH))],
        out_specs=out_specs,
        out_shape=out_shape,
        compiler_params=pltpu.CompilerParams(dimension_semantics=("arbitrary",), vmem_limit_bytes=VMEM_LIMIT),
        name="in_proj_prompt" if prompt else "in_proj_sample",
    )(x, ln1, w_in16, avn, qn, kn, mixw, mixb)


def _top_k_mask(g, valid, lane0, n_cand):
    lane = lax.broadcasted_iota(jnp.int32, g.shape, 1)
    gm = jnp.where(valid, g, NEG_INF)
    rank = jnp.zeros(g.shape, jnp.int32)
    for m in range(lane0, lane0 + n_cand):
        col = gm[:, m:m + 1]
        beats = (col > gm) | ((col == gm) & (lane > m))
        rank = rank + beats.astype(jnp.int32)
    return valid & (rank < MOBA_TOPK)


def _top_k_mask_pair(g, valid, n_cand):
    lane = lax.broadcasted_iota(jnp.int32, g.shape, 1)
    upper = lane >= B_HEAD_DIM
    idx = lane % B_HEAD_DIM
    gm = jnp.where(valid, g, NEG_INF)
    rank = jnp.zeros(g.shape, jnp.int32)
    for m in range(n_cand):
        col = jnp.where(upper, gm[:, B_HEAD_DIM + m:B_HEAD_DIM + m + 1], gm[:, m:m + 1])
        beats = (col > gm) | ((col == gm) & (idx > m))
        rank = rank + beats.astype(jnp.int32)
    return valid & (rank < MOBA_TOPK)


def _moba_prompt_kernel(q_ref, kt_ref, vt_ref, o_ref, kaug_ref, vaug_ref):
    seq = q_ref.shape[0]
    n_blocks = seq // MOBA_BLOCK
    heads = ((0, B_HEAD_DIM), (B_HEAD_DIM, 0))

    kt = kt_ref[...]
    vt = vt_ref[...]
    chan_s = lax.broadcasted_iota(jnp.int32, (LANES, seq), 0)
    blk_s = lax.broadcasted_iota(jnp.int32, (LANES, seq), 1) // MOBA_BLOCK
    lane_c = lax.broadcasted_iota(jnp.int32, (LANES, LANES), 1)
    kmean = [jnp.mean(kt_ref[:, j * MOBA_BLOCK:(j + 1) * MOBA_BLOCK], axis=1, keepdims=True)
             for j in range(n_blocks)]
    chan_c = lax.broadcasted_iota(jnp.int32, (LANES, LANES), 0)
    gate_w = jnp.zeros((LANES, LANES), F32)
    for h, (d0, a0) in enumerate(heads):
        in_head = (chan_s >= d0) & (chan_s < d0 + B_HEAD_DIM)
        kaug_ref[h] = jnp.where(in_head, kt, jnp.where(chan_s - a0 == blk_s, 1.0, 0.0)).astype(BF16)
        vaug_ref[h] = jnp.where(in_head, vt, jnp.where(chan_s == a0, 1.0, 0.0)).astype(BF16)
        in_head_c = (chan_c >= d0) & (chan_c < d0 + B_HEAD_DIM)
        for n in range(n_blocks):
            gate_w = jnp.where((lane_c == a0 + n) & in_head_c, kmean[n], gate_w)
    gate_w = gate_w.astype(BF16)

    lane_q = lax.broadcasted_iota(jnp.int32, (MOBA_BLOCK, LANES), 1)
    kpos = lax.broadcasted_iota(jnp.int32, (MOBA_BLOCK, MOBA_BLOCK), 1)
    qpos = lax.broadcasted_iota(jnp.int32, (MOBA_BLOCK, MOBA_BLOCK), 0)
    causal = kpos <= qpos

    for i in range(n_blocks):
        rows = slice(i * MOBA_BLOCK, (i + 1) * MOBA_BLOCK)
        n_keys = (i + 1) * MOBA_BLOCK
        q_pair = q_ref[rows, :]
        bias = jnp.zeros_like(q_pair)
        if i > MOBA_TOPK:
            past = lane_q % B_HEAD_DIM < i
            sel = _top_k_mask_pair(_dot(q_pair, gate_w), past, i)
            bias = jnp.where(past & ~sel, MASK_BIAS, 0.0).astype(BF16)
        outs = []
        for h, (d0, a0) in enumerate(heads):
            in_head = (lane_q >= d0) & (lane_q < d0 + B_HEAD_DIM)
            q_aug = jnp.where(in_head, q_pair, bias)
            s = _dot(q_aug, kaug_ref[h, :, 0:n_keys])
            s_own = jnp.where(causal, s[:, n_keys - MOBA_BLOCK:], MASK_BIAS)
            s = jnp.concatenate([s[:, :n_keys - MOBA_BLOCK], s_own], axis=-1) if i else s_own
            p = jnp.exp2(s - jnp.max(s, axis=-1, keepdims=True)).astype(BF16)
            acc = _dot_nt(p, vaug_ref[h, :, 0:n_keys])
            outs.append(acc / acc[:, a0:a0 + 1])
        o_ref[rows, :] = jnp.where(lane_q < B_HEAD_DIM, outs[0], outs[1]).astype(o_ref.dtype)


def _moba_sample_kernel(pt_ref, qrep_ref, knew_ref, vnew_ref, ck_ref, cv_ref, o_ref,
                        kbuf_ref, vbuf_ref, sem_ref, s_ref, *, n_past_blocks, while_pages_fly=None):
    b = pl.program_id(0)
    n_pages, _, page = kbuf_ref.shape
    pages_per_block = MOBA_BLOCK // page
    n_rows = qrep_ref.shape[0]

    def page_copy(cache_ref, buf_ref, sem_idx, seq_idx, p):
        return pltpu.make_async_copy(cache_ref.at[pt_ref[seq_idx, p]], buf_ref.at[p], sem_ref.at[sem_idx])

    def for_all_pages(fn):
        lax.fori_loop(0, n_pages, lambda p, c: (fn(p), c)[1], 0)

    @pl.when(b == 0)
    def _first_keys():
        for_all_pages(lambda p: page_copy(ck_ref, kbuf_ref, 0, 0, p).start())

    for_all_pages(lambda p: page_copy(cv_ref, vbuf_ref, 1, b, p).start())

    row = lax.broadcasted_iota(jnp.int32, (n_rows, B_DIM), 0)
    col = lax.broadcasted_iota(jnp.int32, (n_rows, B_DIM), 1)
    own_head = col // B_HEAD_DIM == row % B_HEADS
    q_heads = jnp.where(own_head, qrep_ref[...].astype(F32), 0.0)

    for_all_pages(lambda p: page_copy(ck_ref, kbuf_ref, 0, b, p).wait())
    for p in range(n_pages):
        s_ref[p] = _dot(q_heads, kbuf_ref[p])

    if while_pages_fly is not None:
        while_pages_fly()

    @pl.when(b + 1 < pl.num_programs(0))
    def _next_keys():
        for_all_pages(lambda p: page_copy(ck_ref, kbuf_ref, 0, b + 1, p).start())

    def block_scores(n):
        return [s_ref[pg] for pg in range(n * pages_per_block, (n + 1) * pages_per_block)]

    lane = lax.broadcasted_iota(jnp.int32, (n_rows, LANES), 1)
    qrow = lax.broadcasted_iota(jnp.int32, (n_rows, LANES), 0) // B_HEADS
    gate = jnp.zeros((n_rows, LANES), F32)
    for n in range(n_past_blocks):
        gate = jnp.where(lane == n, jnp.sum(sum(block_scores(n)), axis=-1, keepdims=True), gate)
    sel = _top_k_mask(gate, lane < n_past_blocks, 0, n_past_blocks).astype(F32)

    k_own = jnp.concatenate([knew_ref[...], jnp.zeros((LANES - knew_ref.shape[0], B_DIM), F32)], axis=0)
    own_ok = lane <= qrow
    s_own = jnp.where(own_ok, _dot_nt(q_heads, k_own), NEG_INF)

    sel_cols = [sel[:, n:n + 1] > 0.0 for n in range(n_past_blocks)]
    m_run = jnp.full((n_rows, page), NEG_INF, F32)
    for n in range(n_past_blocks):
        for sn in block_scores(n):
            m_run = jnp.maximum(m_run, jnp.where(sel_cols[n], sn, NEG_INF))
    m = jnp.maximum(jnp.max(m_run, axis=-1, keepdims=True), jnp.max(s_own, axis=-1, keepdims=True))

    p_own = jnp.exp2(s_own - m)
    l_run = jnp.zeros((n_rows, page), F32)
    for pg in range(n_pages):
        pn = jnp.where(sel_cols[pg // pages_per_block], jnp.exp2(s_ref[pg] - m), 0.0)
        s_ref[pg] = pn
        l_run = l_run + pn
    l = jnp.sum(l_run, axis=-1, keepdims=True) + jnp.sum(p_own, axis=-1, keepdims=True)
    acc = jnp.zeros((n_rows, B_DIM), F32)
    for t in range(vnew_ref.shape[0]):
        acc = acc + p_own[:, t:t + 1] * vnew_ref[t:t + 1, :]

    for_all_pages(lambda p: page_copy(cv_ref, vbuf_ref, 1, b, p).wait())

    def value_group(g, acc):
        pv = None
        for r in range(SAMPLE_PAGE_GROUP):
            p = g * SAMPLE_PAGE_GROUP + r
            d = _dot_nt(s_ref[p], vbuf_ref[p])
            pv = d if pv is None else pv + d
        return acc + pv

    acc = lax.fori_loop(0, n_pages // SAMPLE_PAGE_GROUP, value_group, acc)

    out = jnp.where(own_head, acc / l, 0.0)
    n_q = n_rows // B_HEADS
    rows = [jnp.sum(out[t * B_HEADS:(t + 1) * B_HEADS, :], axis=0, keepdims=True) for t in range(n_q)]
    o_ref[...] = jnp.concatenate(rows + [jnp.zeros((o_ref.shape[0] - n_q, B_DIM), F32)], axis=0)


def _moba_kernel(pt_ref, q_ref, kt_ref, vt_ref, qrep_ref, knew_ref, vnew_ref, ck_ref, cv_ref,
                 o_prompt_ref, o_sample_ref, kaug_ref, vaug_ref, kbuf_ref, vbuf_ref, sem_ref, s_ref, *, n_past_blocks):
    _moba_sample_kernel(
        pt_ref, qrep_ref, knew_ref, vnew_ref, ck_ref, cv_ref, o_sample_ref, kbuf_ref, vbuf_ref, sem_ref, s_ref,
        n_past_blocks=n_past_blocks,
        while_pages_fly=functools.partial(_moba_prompt_kernel, q_ref, kt_ref, vt_ref, o_prompt_ref, kaug_ref, vaug_ref))


def _moba(q, kt, vt, page_table, qrep, knew, vnew, cache_k, cache_v, *, batch, seq):
    pairs = B_DIM // LANES
    dec_batch, n_pages = page_table.shape
    page = cache_k.shape[2]
    n_past_blocks = n_pages * page // MOBA_BLOCK
    n_rows, new_pad = qrep.shape[1], knew.shape[1]
    assert n_pages % SAMPLE_PAGE_GROUP == 0 and dec_batch == batch * pairs

    spec = pl.BlockSpec((seq, LANES), lambda i, pt: (i // pairs, i % pairs))
    spec_t = pl.BlockSpec((None, LANES, seq), lambda i, pt: (i // pairs, i % pairs, 0))
    per_seq = lambda rows: pl.BlockSpec((None, rows, B_DIM), lambda i, pt: (i, 0, 0))
    in_hbm = pl.BlockSpec(memory_space=pl.ANY)
    page_buf = pltpu.VMEM((n_pages, B_DIM, page), F32)
    grid_spec = pltpu.PrefetchScalarGridSpec(
        num_scalar_prefetch=1,
        grid=(dec_batch,),
        in_specs=[spec, spec_t, spec_t, per_seq(n_rows), per_seq(new_pad), per_seq(new_pad), in_hbm, in_hbm],
        out_specs=[spec, per_seq(new_pad)],
        scratch_shapes=[pltpu.VMEM((2, LANES, seq), BF16), pltpu.VMEM((2, LANES, seq), BF16),
                        page_buf, page_buf, pltpu.SemaphoreType.DMA((2,)), pltpu.VMEM((n_pages, n_rows, page), F32)],
    )
    return pl.pallas_call(
        functools.partial(_moba_kernel, n_past_blocks=n_past_blocks),
        grid_spec=grid_spec,
        out_shape=[jax.ShapeDtypeStruct((batch * seq, B_DIM), BF16),
                   jax.ShapeDtypeStruct((dec_batch, new_pad, B_DIM), F32)],
        compiler_params=pltpu.CompilerParams(dimension_semantics=("arbitrary",), vmem_limit_bytes=VMEM_LIMIT),
        name="moba",
    )(page_table, q, kt, vt, qrep, knew, vnew, cache_k, cache_v)


def _mlp_ple_kernel(x_ref, ma_ref, mb_ref, p_ref, wout_ref, ln2_ref, wup_ref, wdn_ref, ln3_ref, wg_ref,
                    wpp_ref, pn_ref, y_ref, mix_ref, act_ref):
    mix_ref[:, 0:A_DIM] = ma_ref[...]
    mix_ref[:, A_DIM:A_DIM + B_DIM] = mb_ref[...]
    h = x_ref[...] + _dot(mix_ref[...], wout_ref[...])
    hn = _rms(h, ln2_ref[...]).astype(BF16)
    for c0 in range(0, D_FF, FF_CHUNK):
        act_ref[:, c0:c0 + FF_CHUNK] = jnp.square(jnp.maximum(_dot(hn, wup_ref[:, c0:c0 + FF_CHUNK]), 0.0)).astype(BF16)
    h = h + _dot(act_ref[...], wdn_ref[...])
    gate = 1.0 / (1.0 + jnp.exp(-_dot(_rms(h, ln3_ref[...]).astype(BF16), wg_ref[...])))
    e = _rms(_dot(p_ref[...].astype(BF16), wpp_ref[...]), pn_ref[...])
    y_ref[...] = h + gate * e


def _mlp_ple(x, mix_a, mix_b, p, w_out16, ln2, w_up16, w_down16, ln3, w_gate16, w_proj16, ple_norm, *,
             rows_per_step, name):
    n = x.shape[0]
    row = lambda width: pl.BlockSpec((rows_per_step, width), lambda i: (i, 0))
    return pl.pallas_call(
        _mlp_ple_kernel,
        grid=(n // rows_per_step,),
        in_specs=[row(D_MODEL), row(A_DIM), row(B_DIM), row(D_PLE),
                  _const_spec((A_DIM + B_DIM, D_MODEL)), _const_spec((1, D_MODEL)),
                  _const_spec((D_MODEL, D_FF)), _const_spec((D_FF, D_MODEL)), _const_spec((1, D_MODEL)),
                  _const_spec((D_MODEL, D_MODEL)), _const_spec((D_PLE, D_MODEL)), _const_spec((1, D_MODEL))],
        out_specs=row(D_MODEL),
        out_shape=jax.ShapeDtypeStruct((n, D_MODEL), F32),
        scratch_shapes=[pltpu.VMEM((rows_per_step, A_DIM + B_DIM), BF16), pltpu.VMEM((rows_per_step, D_FF), BF16)],
        compiler_params=pltpu.CompilerParams(dimension_semantics=("arbitrary",), vmem_limit_bytes=VMEM_LIMIT),
        name=name,
    )(x, mix_a, mix_b, p, w_out16, ln2, w_up16, w_down16, ln3, w_gate16, w_proj16, ple_norm)


def kernel(x_prompt, x_sample, p_prompt, p_sample, cache_k, cache_v, page_table, ln1, w_in, a_v_norm, a_ws, a_bs,
           q_norm, k_norm, w_out, ln2, w_up, w_down, ln3, w_ple_gate, w_ple_proj, ple_norm):
    depth = ln1.shape[0]
    batch, seq, _ = x_prompt.shape
    dec_batch, n_new, _ = x_sample.shape
    n_pool, page = cache_k.shape[1], cache_k.shape[2]
    n_p, n_s = batch * seq, dec_batch * n_new
    assert n_s == A_CHUNK and n_new <= A_CHUNK and A_CHUNK % n_new == 0 and seq % IN_PROJ_ROW_TILE == 0
    assert MOBA_BLOCK % page == 0 and (page_table.shape[1] * page) % MOBA_BLOCK == 0

    hp = x_prompt.reshape(n_p, D_MODEL)
    hs = x_sample.reshape(n_s, D_MODEL)
    kp_l, vp_l, ks_l, vs_l, as_l = [], [], [], [], []
    for i in range(depth):
        row = lambda a: a.reshape(1, -1)
        w_in16, w_out16, w_up16, w_down16 = (w[i].astype(BF16) for w in (w_in, w_out, w_up, w_down))
        w_gate16, w_proj16 = w_ple_gate[i].astype(BF16), w_ple_proj[i].astype(BF16)
        avn = row(a_v_norm[i])
        qn, kn = row(jnp.tile(q_norm[i], B_HEADS)), row(jnp.tile(k_norm[i], B_HEADS))
        mix_bias = jnp.broadcast_to(a_bs[i][:, :, None], (A_GROUPS, A_CHUNK, A_WIDTH))

        mix_a, q, kt, vt = _in_proj(hp, row(ln1[i]), w_in16, avn, qn, kn, a_ws[i], mix_bias,
                                    rows_per_step=IN_PROJ_ROW_TILE, mix_block=A_CHUNK, seq=seq)
        mix_as, qs, k, v, va = _in_proj(hs, row(ln1[i]), w_in16, avn, qn, kn, a_ws[i], mix_bias,
                                        rows_per_step=n_s, mix_block=n_new)

        new_pad = 8
        pad_new = lambda a: jnp.pad(a.reshape(dec_batch, n_new, B_DIM), ((0, 0), (0, new_pad - n_new), (0, 0)))
        qrep = jnp.repeat(qs.reshape(dec_batch, n_new, B_DIM), B_HEADS, axis=1)
        pages_t = lambda c: c.transpose(0, 2, 3, 1).reshape(n_pool, B_DIM, page)
        mix_b, o = _moba(q, kt, vt, page_table, qrep, pad_new(k), pad_new(v), pages_t(cache_k[i]),
                         pages_t(cache_v[i]), batch=batch, seq=seq)
        mix_bs = o[:, :n_new, :].reshape(n_s, B_DIM).astype(BF16)

        hp = _mlp_ple(hp, mix_a, mix_b, p_prompt[i].reshape(n_p, D_PLE), w_out16, row(ln2[i]), w_up16, w_down16,
                      row(ln3[i]), w_gate16, w_proj16, row(ple_norm[i]), rows_per_step=ROW_TILE, name="mlp_ple_prompt")
        hs = _mlp_ple(hs, mix_as, mix_bs, p_sample[i].reshape(n_s, D_PLE), w_out16, row(ln2[i]), w_up16, w_down16,
                      row(ln3[i]), w_gate16, w_proj16, row(ple_norm[i]), rows_per_step=n_s, name="mlp_ple_sample")
        heads_last = lambda t: t.reshape(batch, B_HEADS, B_HEAD_DIM, seq).transpose(0, 3, 1, 2)
        kp_l.append(heads_last(kt))
        vp_l.append(heads_last(vt))
        ks_l.append(k.reshape(dec_batch, n_new, B_HEADS, B_HEAD_DIM))
        vs_l.append(v.reshape(dec_batch, n_new, B_HEADS, B_HEAD_DIM))
        as_l.append(va.reshape(dec_batch, n_new, A_GROUPS, A_WIDTH))

    return (hp.reshape(batch, seq, D_MODEL), hs.reshape(dec_batch, n_new, D_MODEL),
            jnp.stack(kp_l), jnp.stack(vp_l), jnp.stack(ks_l), jnp.stack(vs_l), jnp.stack(as_l))
```

```python
import functools

import jax
import jax.numpy as jnp
from jax import lax
from jax.experimental import pallas as pl
from jax.experimental.pallas import tpu as pltpu

D_MODEL = 1024
D_PLE = 256
A_GROUPS = 4
A_WIDTH = 128
A_CHUNK = 128
A_DIM = A_GROUPS * A_WIDTH
B_HEADS = 8
B_HEAD_DIM = 64
B_DIM = B_HEADS * B_HEAD_DIM
MOBA_BLOCK = 256
MOBA_TOPK = 3
D_FF = 4 * D_MODEL
IN_DIM = 2 * A_DIM + 3 * B_DIM
EPS = 1e-6

LANES = 128
ROW_TILE = 512
IN_PROJ_ROW_TILE = 1024
FF_CHUNK = 1024
SAMPLE_PAGE_GROUP = 8
VMEM_LIMIT = 56 * 1024 * 1024

BF16 = jnp.bfloat16
F32 = jnp.float32
NEG_INF = float("-inf")
LOG2_E = 1.4426950408889634
MASK_BIAS = -1e30


def _dot(a, b):
    return jnp.dot(a, b, preferred_element_type=F32)


def _dot_nt(a, b):
    return lax.dot_general(a, b, (((1,), (1,)), ((), ())), preferred_element_type=F32)


def _rms(x, gain):
    return x * lax.rsqrt(jnp.mean(x * x, axis=-1, keepdims=True) + EPS) * gain


def _const_spec(shape):
    nd = len(shape)
    return pl.BlockSpec(shape, lambda *_: (0,) * nd, pipeline_mode=pl.Buffered(1))


def _head_rms(z, gain_row):
    lane = lax.broadcasted_iota(jnp.int32, (1, LANES), 1)
    lo = lane < B_HEAD_DIM
    outs = []
    for t in range(B_DIM // LANES):
        zt = z[:, t * LANES:(t + 1) * LANES]
        z2 = zt * zt
        s_lo = jnp.sum(jnp.where(lo, z2, 0.0), axis=-1, keepdims=True)
        s_hi = jnp.sum(jnp.where(lo, 0.0, z2), axis=-1, keepdims=True)
        ms = jnp.where(lo, s_lo, s_hi) * (1.0 / B_HEAD_DIM)
        outs.append(zt * lax.rsqrt(ms + EPS))
    return jnp.concatenate(outs, axis=-1) * gain_row


def _tile_corner(m, blk, tile_lanes):
    r = lax.broadcasted_iota(jnp.int32, m.shape, 0) % blk
    c = lax.broadcasted_iota(jnp.int32, m.shape, 1) % blk
    out = jnp.zeros_like(m)
    for t in range(blk):
        out = jnp.where(r == t, m[t:t + 1, :], out)
    if tile_lanes:
        by_rows, out = out, jnp.zeros_like(m)
        for t in range(blk):
            out = jnp.where(c == t, by_rows[:, t:t + 1], out)
    return out


def _in_proj_kernel(x_ref, ln1_ref, w_ref, avn_ref, qn_ref, kn_ref, mixw_ref, mixb_ref,
                    *out_refs, mix_block, prompt):
    if prompt:
        mixa_ref, q_ref, k_ref, v_ref = out_refs
    else:
        mixa_ref, q_ref, k_ref, v_ref, va_ref = out_refs
    rows = x_ref.shape[0]
    hn = _rms(x_ref[...], ln1_ref[...]).astype(BF16)

    def proj(c0, n):
        return _dot(hn, w_ref[:, c0:c0 + n])

    u = jax.nn.gelu(proj(0, A_DIM))
    gv = jax.nn.gelu(proj(A_DIM, A_DIM))
    va = jnp.concatenate(
        [_rms(gv[:, g * A_WIDTH:(g + 1) * A_WIDTH], 1.0) for g in range(A_GROUPS)], axis=-1) * avn_ref[...]
    if not prompt:
        va_ref[...] = va

    r = lax.broadcasted_iota(jnp.int32, (A_CHUNK, A_CHUNK), 0)
    c = lax.broadcasted_iota(jnp.int32, (A_CHUNK, A_CHUNK), 1)
    keep = (r // mix_block == c // mix_block) & (c % mix_block <= r % mix_block)
    va16 = va.astype(BF16)
    for g in range(A_GROUPS):
        wg, bg = mixw_ref[g], mixb_ref[g]
        if mix_block != A_CHUNK:
            wg, bg = _tile_corner(wg, mix_block, True), _tile_corner(bg, mix_block, False)
        wg = jnp.where(keep, wg, 0.0).astype(BF16)
        for ch in range(rows // A_CHUNK):
            rs = slice(ch * A_CHUNK, (ch + 1) * A_CHUNK)
            cs = slice(g * A_WIDTH, (g + 1) * A_WIDTH)
            sg = _dot(wg, va16[rs, cs]) + bg
            mixa_ref[rs, cs] = (u[rs, cs] * sg).astype(BF16)

    q0 = 2 * A_DIM
    q = _head_rms(proj(q0, B_DIM), qn_ref[...])
    q_ref[...] = (q * (B_HEAD_DIM ** -0.5 * LOG2_E)).astype(BF16)
    k = _head_rms(proj(q0 + B_DIM, B_DIM), kn_ref[...])
    v = proj(q0 + 2 * B_DIM, B_DIM)
    if prompt:
        k_ref[...] = k.T
        v_ref[...] = v.T
    else:
        k_ref[...] = k
        v_ref[...] = v


def _in_proj(x, ln1, w_in16, avn, qn, kn, mixw, mixb, *, rows_per_step, mix_block, seq=None):
    n = x.shape[0]
    prompt = seq is not None
    row = lambda width: pl.BlockSpec((rows_per_step, width), lambda i: (i, 0))
    out_shape = [jax.ShapeDtypeStruct((n, A_DIM), BF16), jax.ShapeDtypeStruct((n, B_DIM), BF16)]
    out_specs = [row(A_DIM), row(B_DIM)]
    if prompt:
        steps = seq // rows_per_step
        kv_t = pl.BlockSpec((None, B_DIM, rows_per_step), lambda i: (i // steps, 0, i % steps))
        out_shape += [jax.ShapeDtypeStruct((n // seq, B_DIM, seq), F32)] * 2
        out_specs += [kv_t, kv_t]
    else:
        out_shape += [jax.ShapeDtypeStruct((n, B_DIM), F32)] * 2 + [jax.ShapeDtypeStruct((n, A_DIM), F32)]
        out_specs += [row(B_DIM), row(B_DIM), row(A_DIM)]
    return pl.pallas_call(
        functools.partial(_in_proj_kernel, mix_block=mix_block, prompt=prompt),
        grid=(n // rows_per_step,),
        in_specs=[row(D_MODEL), _const_spec((1, D_MODEL)), _const_spec((D_MODEL, IN_DIM)),
                  _const_spec((1, A_DIM)), _const_spec((1, B_DIM)), _const_spec((1, B_DIM)),
                  _const_spec((A_GROUPS, A_CHUNK, A_CHUNK)), _const_spec((A_GROUPS, A_CHUNK, A_WIDTH))],
        out_specs=out_specs,
        out_shape=out_shape,
        compiler_params=pltpu.CompilerParams(dimension_semantics=("arbitrary",), vmem_limit_bytes=VMEM_LIMIT),
        name="in_proj_prompt" if prompt else "in_proj_sample",
    )(x, ln1, w_in16, avn, qn, kn, mixw, mixb)


def _top_k_mask(g, valid, lane0, n_cand):
    lane = lax.broadcasted_iota(jnp.int32, g.shape, 1)
    gm = jnp.where(valid, g, NEG_INF)
    rank = jnp.zeros(g.shape, jnp.int32)
    for m in range(lane0, lane0 + n_cand):
        col = gm[:, m:m + 1]
        beats = (col > gm) | ((col == gm) & (lane > m))
        rank = rank + beats.astype(jnp.int32)
    return valid & (rank < MOBA_TOPK)


def _top_k_mask_pair(g, valid, n_cand):
    lane = lax.broadcasted_iota(jnp.int32, g.shape, 1)
    upper = lane >= B_HEAD_DIM
    idx = lane % B_HEAD_DIM
    gm = jnp.where(valid, g, NEG_INF)
    rank = jnp.zeros(g.shape, jnp.int32)
    for m in range(n_cand):
        col = jnp.where(upper, gm[:, B_HEAD_DIM + m:B_HEAD_DIM + m + 1], gm[:, m:m + 1])
        beats = (col > gm) | ((col == gm) & (idx > m))
        rank = rank + beats.astype(jnp.int32)
    return valid & (rank < MOBA_TOPK)


def _moba_prompt_kernel(q_ref, kt_ref, vt_ref, o_ref, kaug_ref, vaug_ref):
    seq = q_ref.shape[0]
    n_blocks = seq // MOBA_BLOCK
    heads = ((0, B_HEAD_DIM), (B_HEAD_DIM, 0))

    kt = kt_ref[...]
    vt = vt_ref[...]
    chan_s = lax.broadcasted_iota(jnp.int32, (LANES, seq), 0)
    blk_s = lax.broadcasted_iota(jnp.int32, (LANES, seq), 1) // MOBA_BLOCK
    lane_c = lax.broadcasted_iota(jnp.int32, (LANES, LANES), 1)
    kmean = [jnp.mean(kt_ref[:, j * MOBA_BLOCK:(j + 1) * MOBA_BLOCK], axis=1, keepdims=True)
             for j in range(n_blocks)]
    chan_c = lax.broadcasted_iota(jnp.int32, (LANES, LANES), 0)
    gate_w = jnp.zeros((LANES, LANES), F32)
    for h, (d0, a0) in enumerate(heads):
        in_head = (chan_s >= d0) & (chan_s < d0 + B_HEAD_DIM)
        kaug_ref[h] = jnp.where(in_head, kt, jnp.where(chan_s - a0 == blk_s, 1.0, 0.0)).astype(BF16)
        vaug_ref[h] = jnp.where(in_head, vt, jnp.where(chan_s == a0, 1.0, 0.0)).astype(BF16)
        in_head_c = (chan_c >= d0) & (chan_c < d0 + B_HEAD_DIM)
        for n in range(n_blocks):
            gate_w = jnp.where((lane_c == a0 + n) & in_head_c, kmean[n], gate_w)
    gate_w = gate_w.astype(BF16)

    lane_q = lax.broadcasted_iota(jnp.int32, (MOBA_BLOCK, LANES), 1)
    kpos = lax.broadcasted_iota(jnp.int32, (MOBA_BLOCK, MOBA_BLOCK), 1)
    qpos = lax.broadcasted_iota(jnp.int32, (MOBA_BLOCK, MOBA_BLOCK), 0)
    causal = kpos <= qpos

    for i in range(n_blocks):
        rows = slice(i * MOBA_BLOCK, (i + 1) * MOBA_BLOCK)
        n_keys = (i + 1) * MOBA_BLOCK
        q_pair = q_ref[rows, :]
        bias = jnp.zeros_like(q_pair)
        if i > MOBA_TOPK:
            past = lane_q % B_HEAD_DIM < i
            sel = _top_k_mask_pair(_dot(q_pair, gate_w), past, i)
            bias = jnp.where(past & ~sel, MASK_BIAS, 0.0).astype(BF16)
        outs = []
        for h, (d0, a0) in enumerate(heads):
            in_head = (lane_q >= d0) & (lane_q < d0 + B_HEAD_DIM)
            q_aug = jnp.where(in_head, q_pair, bias)
            s = _dot(q_aug, kaug_ref[h, :, 0:n_keys])
            s_own = jnp.where(causal, s[:, n_keys - MOBA_BLOCK:], MASK_BIAS)
            s = jnp.concatenate([s[:, :n_keys - MOBA_BLOCK], s_own], axis=-1) if i else s_own
            p = jnp.exp2(s - jnp.max(s, axis=-1, keepdims=True)).astype(BF16)
            acc = _dot_nt(p, vaug_ref[h, :, 0:n_keys])
            outs.append(acc / acc[:, a0:a0 + 1])
        o_ref[rows, :] = jnp.where(lane_q < B_HEAD_DIM, outs[0], outs[1]).astype(o_ref.dtype)


def _moba_sample_kernel(pt_ref, qrep_ref, knew_ref, vnew_ref, ck_ref, cv_ref, o_ref,
                        kbuf_ref, vbuf_ref, sem_ref, s_ref, *, n_past_blocks, while_pages_fly=None):
    b = pl.program_id(0)
    n_pages, _, page = kbuf_ref.shape
    pages_per_block = MOBA_BLOCK // page
    n_rows = qrep_ref.shape[0]

    def page_copy(cache_ref, buf_ref, sem_idx, seq_idx, p):
        return pltpu.make_async_copy(cache_ref.at[pt_ref[seq_idx, p]], buf_ref.at[p], sem_ref.at[sem_idx])

    def for_all_pages(fn):
        lax.fori_loop(0, n_pages, lambda p, c: (fn(p), c)[1], 0)

    @pl.when(b == 0)
    def _first_keys():
        for_all_pages(lambda p: page_copy(ck_ref, kbuf_ref, 0, 0, p).start())

    for_all_pages(lambda p: page_copy(cv_ref, vbuf_ref, 1, b, p).start())

    row = lax.broadcasted_iota(jnp.int32, (n_rows, B_DIM), 0)
    col = lax.broadcasted_iota(jnp.int32, (n_rows, B_DIM), 1)
    own_head = col // B_HEAD_DIM == row % B_HEADS
    q_heads = jnp.where(own_head, qrep_ref[...].astype(F32), 0.0)

    for_all_pages(lambda p: page_copy(ck_ref, kbuf_ref, 0, b, p).wait())
    for p in range(n_pages):
        s_ref[p] = _dot(q_heads, kbuf_ref[p])

    if while_pages_fly is not None:
        while_pages_fly()

    def block_scores(n):
        return [s_ref[pg] for pg in range(n * pages_per_block, (n + 1) * pages_per_block)]

    lane = lax.broadcasted_iota(jnp.int32, (n_rows, LANES), 1)
    qrow = lax.broadcasted_iota(jnp.int32, (n_rows, LANES), 0) // B_HEADS
    gate = jnp.zeros((n_rows, LANES), F32)
    for n in range(n_past_blocks):
        gate = jnp.where(lane == n, jnp.sum(sum(block_scores(n)), axis=-1, keepdims=True), gate)
    sel = _top_k_mask(gate, lane < n_past_blocks, 0, n_past_blocks).astype(F32)

    k_own = jnp.concatenate([knew_ref[...], jnp.zeros((LANES - knew_ref.shape[0], B_DIM), F32)], axis=0)
    own_ok = lane <= qrow
    s_own = jnp.where(own_ok, _dot_nt(q_heads, k_own), NEG_INF)

    sel_cols = [sel[:, n:n + 1] > 0.0 for n in range(n_past_blocks)]
    m_run = jnp.full((n_rows, page), NEG_INF, F32)
    for n in range(n_past_blocks):
        for sn in block_scores(n):
            m_run = jnp.maximum(m_run, jnp.where(sel_cols[n], sn, NEG_INF))
    m = jnp.maximum(jnp.max(m_run, axis=-1, keepdims=True), jnp.max(s_own, axis=-1, keepdims=True))

    p_own = jnp.exp2(s_own - m)
    l_run = jnp.zeros((n_rows, page), F32)
    for pg in range(n_pages):
        pn = jnp.where(sel_cols[pg // pages_per_block], jnp.exp2(s_ref[pg] - m), 0.0)
        s_ref[pg] = pn
        l_run = l_run + pn
    l = jnp.sum(l_run, axis=-1, keepdims=True) + jnp.sum(p_own, axis=-1, keepdims=True)
    acc = jnp.zeros((n_rows, B_DIM), F32)
    for t in range(vnew_ref.shape[0]):
        acc = acc + p_own[:, t:t + 1] * vnew_ref[t:t + 1, :]

    @pl.when(b + 1 < pl.num_programs(0))
    def _next_keys():
        for_all_pages(lambda p: page_copy(ck_ref, kbuf_ref, 0, b + 1, p).start())

    for_all_pages(lambda p: page_copy(cv_ref, vbuf_ref, 1, b, p).wait())

    def value_group(g, acc):
        pv = None
        for r in range(SAMPLE_PAGE_GROUP):
            p = g * SAMPLE_PAGE_GROUP + r
            d = _dot_nt(s_ref[p], vbuf_ref[p])
            pv = d if pv is None else pv + d
        return acc + pv

    acc = lax.fori_loop(0, n_pages // SAMPLE_PAGE_GROUP, value_group, acc)

    out = jnp.where(own_head, acc / l, 0.0)
    n_q = n_rows // B_HEADS
    rows = [jnp.sum(out[t * B_HEADS:(t + 1) * B_HEADS, :], axis=0, keepdims=True) for t in range(n_q)]
    o_ref[...] = jnp.concatenate(rows + [jnp.zeros((o_ref.shape[0] - n_q, B_DIM), F32)], axis=0)


def _moba_kernel(pt_ref, q_ref, kt_ref, vt_ref, qrep_ref, knew_ref, vnew_ref, ck_ref, cv_ref,
                 o_prompt_ref, o_sample_ref, kaug_ref, vaug_ref, kbuf_ref, vbuf_ref, sem_ref, s_ref, *, n_past_blocks):
    _moba_sample_kernel(
        pt_ref, qrep_ref, knew_ref, vnew_ref, ck_ref, cv_ref, o_sample_ref, kbuf_ref, vbuf_ref, sem_ref, s_ref,
        n_past_blocks=n_past_blocks,
        while_pages_fly=functools.partial(_moba_prompt_kernel, q_ref, kt_ref, vt_ref, o_prompt_ref, kaug_ref, vaug_ref))


def _moba(q, kt, vt, page_table, qrep, knew, vnew, cache_k, cache_v, *, batch, seq):
    pairs = B_DIM // LANES
    dec_batch, n_pages = page_table.shape
    page = cache_k.shape[2]
    n_past_blocks = n_pages * page // MOBA_BLOCK
    n_rows, new_pad = qrep.shape[1], knew.shape[1]
    assert n_pages % SAMPLE_PAGE_GROUP == 0 and dec_batch == batch * pairs

    spec = pl.BlockSpec((seq, LANES), lambda i, pt: (i // pairs, i % pairs))
    spec_t = pl.BlockSpec((None, LANES, seq), lambda i, pt: (i // pairs, i % pairs, 0))
    per_seq = lambda rows: pl.BlockSpec((None, rows, B_DIM), lambda i, pt: (i, 0, 0))
    in_hbm = pl.BlockSpec(memory_space=pl.ANY)
    page_buf = pltpu.VMEM((n_pages, B_DIM, page), F32)
    grid_spec = pltpu.PrefetchScalarGridSpec(
        num_scalar_prefetch=1,
        grid=(dec_batch,),
        in_specs=[spec, spec_t, spec_t, per_seq(n_rows), per_seq(new_pad), per_seq(new_pad), in_hbm, in_hbm],
        out_specs=[spec, per_seq(new_pad)],
        scratch_shapes=[pltpu.VMEM((2, LANES, seq), BF16), pltpu.VMEM((2, LANES, seq), BF16),
                        page_buf, page_buf, pltpu.SemaphoreType.DMA((2,)), pltpu.VMEM((n_pages, n_rows, page), F32)],
    )
    return pl.pallas_call(
        functools.partial(_moba_kernel, n_past_blocks=n_past_blocks),
        grid_spec=grid_spec,
        out_shape=[jax.ShapeDtypeStruct((batch * seq, B_DIM), BF16),
                   jax.ShapeDtypeStruct((dec_batch, new_pad, B_DIM), F32)],
        compiler_params=pltpu.CompilerParams(dimension_semantics=("arbitrary",), vmem_limit_bytes=VMEM_LIMIT),
        name="moba",
    )(page_table, q, kt, vt, qrep, knew, vnew, cache_k, cache_v)


def _mlp_ple_kernel(x_ref, ma_ref, mb_ref, p_ref, wout_ref, ln2_ref, wup_ref, wdn_ref, ln3_ref, wg_ref,
                    wpp_ref, pn_ref, y_ref, mix_ref, act_ref):
    mix_ref[:, 0:A_DIM] = ma_ref[...]
    mix_ref[:, A_DIM:A_DIM + B_DIM] = mb_ref[...]
    h = x_ref[...] + _dot(mix_ref[...], wout_ref[...])
    hn = _rms(h, ln2_ref[...]).astype(BF16)
    for c0 in range(0, D_FF, FF_CHUNK):
        act_ref[:, c0:c0 + FF_CHUNK] = jnp.square(jnp.maximum(_dot(hn, wup_ref[:, c0:c0 + FF_CHUNK]), 0.0)).astype(BF16)
    h = h + _dot(act_ref[...], wdn_ref[...])
    gate = 1.0 / (1.0 + jnp.exp(-_dot(_rms(h, ln3_ref[...]).astype(BF16), wg_ref[...])))
    e = _rms(_dot(p_ref[...].astype(BF16), wpp_ref[...]), pn_ref[...])
    y_ref[...] = h + gate * e


def _mlp_ple(x, mix_a, mix_b, p, w_out16, ln2, w_up16, w_down16, ln3, w_gate16, w_proj16, ple_norm, *,
             rows_per_step, name):
    n = x.shape[0]
    row = lambda width: pl.BlockSpec((rows_per_step, width), lambda i: (i, 0))
    return pl.pallas_call(
        _mlp_ple_kernel,
        grid=(n // rows_per_step,),
        in_specs=[row(D_MODEL), row(A_DIM), row(B_DIM), row(D_PLE),
                  _const_spec((A_DIM + B_DIM, D_MODEL)), _const_spec((1, D_MODEL)),
                  _const_spec((D_MODEL, D_FF)), _const_spec((D_FF, D_MODEL)), _const_spec((1, D_MODEL)),
                  _const_spec((D_MODEL, D_MODEL)), _const_spec((D_PLE, D_MODEL)), _const_spec((1, D_MODEL))],
        out_specs=row(D_MODEL),
        out_shape=jax.ShapeDtypeStruct((n, D_MODEL), F32),
        scratch_shapes=[pltpu.VMEM((rows_per_step, A_DIM + B_DIM), BF16), pltpu.VMEM((rows_per_step, D_FF), BF16)],
        compiler_params=pltpu.CompilerParams(dimension_semantics=("arbitrary",), vmem_limit_bytes=VMEM_LIMIT),
        name=name,
    )(x, mix_a, mix_b, p, w_out16, ln2, w_up16, w_down16, ln3, w_gate16, w_proj16, ple_norm)


def kernel(x_prompt, x_sample, p_prompt, p_sample, cache_k, cache_v, page_table, ln1, w_in, a_v_norm, a_ws, a_bs,
           q_norm, k_norm, w_out, ln2, w_up, w_down, ln3, w_ple_gate, w_ple_proj, ple_norm):
    depth = ln1.shape[0]
    batch, seq, _ = x_prompt.shape
    dec_batch, n_new, _ = x_sample.shape
    n_pool, page = cache_k.shape[1], cache_k.shape[2]
    n_p, n_s = batch * seq, dec_batch * n_new
    assert n_s == A_CHUNK and n_new <= A_CHUNK and A_CHUNK % n_new == 0 and seq % IN_PROJ_ROW_TILE == 0
    assert MOBA_BLOCK % page == 0 and (page_table.shape[1] * page) % MOBA_BLOCK == 0

    hp = x_prompt.reshape(n_p, D_MODEL)
    hs = x_sample.reshape(n_s, D_MODEL)
    kp_l, vp_l, ks_l, vs_l, as_l = [], [], [], [], []
    for i in range(depth):
        row = lambda a: a.reshape(1, -1)
        w_in16, w_out16, w_up16, w_down16 = (w[i].astype(BF16) for w in (w_in, w_out, w_up, w_down))
        w_gate16, w_proj16 = w_ple_gate[i].astype(BF16), w_ple_proj[i].astype(BF16)
        avn = row(a_v_norm[i])
        qn, kn = row(jnp.tile(q_norm[i], B_HEADS)), row(jnp.tile(k_norm[i], B_HEADS))
        mix_bias = jnp.broadcast_to(a_bs[i][:, :, None], (A_GROUPS, A_CHUNK, A_WIDTH))

        mix_a, q, kt, vt = _in_proj(hp, row(ln1[i]), w_in16, avn, qn, kn, a_ws[i], mix_bias,
                                    rows_per_step=IN_PROJ_ROW_TILE, mix_block=A_CHUNK, seq=seq)
        mix_as, qs, k, v, va = _in_proj(hs, row(ln1[i]), w_in16, avn, qn, kn, a_ws[i], mix_bias,
                                        rows_per_step=n_s, mix_block=n_new)

        new_pad = 8
        pad_new = lambda a: jnp.pad(a.reshape(dec_batch, n_new, B_DIM), ((0, 0), (0, new_pad - n_new), (0, 0)))
        qrep = jnp.repeat(qs.reshape(dec_batch, n_new, B_DIM), B_HEADS, axis=1)
        pages_t = lambda c: c.transpose(0, 2, 3, 1).reshape(n_pool, B_DIM, page)
        mix_b, o = _moba(q, kt, vt, page_table, qrep, pad_new(k), pad_new(v), pages_t(cache_k[i]),
                         pages_t(cache_v[i]), batch=batch, seq=seq)
        mix_bs = o[:, :n_new, :].reshape(n_s, B_DIM).astype(BF16)

        hp = _mlp_ple(hp, mix_a, mix_b, p_prompt[i].reshape(n_p, D_PLE), w_out16, row(ln2[i]), w_up16, w_down16,
                      row(ln3[i]), w_gate16, w_proj16, row(ple_norm[i]), rows_per_step=ROW_TILE, name="mlp_ple_prompt")
        hs = _mlp_ple(hs, mix_as, mix_bs, p_sample[i].reshape(n_s, D_PLE), w_out16, row(ln2[i]), w_up16, w_down16,
                      row(ln3[i]), w_gate16, w_proj16, row(ple_norm[i]), rows_per_step=n_s, name="mlp_ple_sample")
        heads_last = lambda t: t.reshape(batch, B_HEADS, B_HEAD_DIM, seq).transpose(0, 3, 1, 2)
        kp_l.append(heads_last(kt))
        vp_l.append(heads_last(vt))
        ks_l.append(k.reshape(dec_batch, n_new, B_HEADS, B_HEAD_DIM))
        vs_l.append(v.reshape(dec_batch, n_new, B_HEADS, B_HEAD_DIM))
        as_l.append(va.reshape(dec_batch, n_new, A_GROUPS, A_WIDTH))

    return (hp.reshape(batch, seq, D_MODEL), hs.reshape(dec_batch, n_new, D_MODEL),
            jnp.stack(kp_l), jnp.stack(vp_l), jnp.stack(ks_l), jnp.stack(vs_l), jnp.stack(as_l))
```
